```python
import math
import jax, jax.numpy as jnp
from jax import lax
import numpy as np


D_MODEL = 1024
BATCH = 8
SEQ = 2048
DEPTH = 4

N_MEM = 256
EPS = 1e-6

A_WIDTH = 512
CONV_WIDTH = 3
POOL_WINDOWS = (2, 4, 8, 16)
POOL_GROUP = 128
B_WIDTH = 512
C_HEADS = 8
C_LATENT = 128
C_HEAD_DIM = 64
IDX_HEADS = 8
IDX_DIM = 64
DSA_TOPK_MAX = 256
Q_BLOCK = 128
REL_BUCKETS = 32
REL_MAX_DIST = 128
D_HEADS = 4
D_KEY = 128
D_VAL = 128
CHUNK = 64
X_HEADS = 4
X_HEAD_DIM = 256
N_KEYS = 128
N_EXPERTS = 16384
PEER_HEADS = 8
PEER_QDIM = 256
PEER_HALF = 128
PEER_TOPK = 16
PEER_BLOCK = 128

EVEN_SPLITS = (512, 512, 512, 512)
ODD_SPLITS = (1024, 128, 512, 64, 8, 512, 512, 512, 512)
EVEN_IN = 2048
ODD_IN = 3784
MIX_WIDTH = 1024

kernel_name = 'hybrid_conv_pool_dsa_hgrn2_peer_trunk'


def _split(a, sizes):
    offs = np.cumsum(sizes)[:-1].tolist()
    return jnp.split(a, offs, axis=-1)


def rmsnorm(x, g):
    xf = x.astype(jnp.float32)
    y = xf * lax.rsqrt(jnp.mean(xf * xf, axis=-1, keepdims=True) + EPS)
    return (y * g.astype(jnp.float32)).astype(x.dtype)


def softmax32(logits, dtype):
    return jax.nn.softmax(logits.astype(jnp.float32), axis=-1).astype(dtype)


def t5_bucket(dist):
    max_exact = REL_BUCKETS // 2
    d = jnp.maximum(dist, 0)
    log_ratio = jnp.log(jnp.maximum(d, 1).astype(jnp.float32) / max_exact) / math.log(REL_MAX_DIST / max_exact)
    large = max_exact + (log_ratio * (REL_BUCKETS - max_exact)).astype(jnp.int32)
    large = jnp.minimum(large, REL_BUCKETS - 1)
    return jnp.where(d < max_exact, d, large)


def short_conv_pool_mixer(h, w_in, conv_w, pool_w, pool_scale, w_out):
    B_, S_, _ = h.shape
    u, gate_c, gate_b, pv = _split(h @ w_in, EVEN_SPLITS)
    z = lax.conv_general_dilated(gate_c * u, conv_w, window_strides=(1,), padding=[(CONV_WIDTH - 1, 0)],
                                 dimension_numbers=('NWC', 'WIO', 'NWC'), feature_group_count=A_WIDTH)
    y_a = gate_b * z
    cs = jnp.cumsum(pv.astype(jnp.float32), axis=1)
    cs = jnp.concatenate([jnp.zeros((B_, 1, B_WIDTH), jnp.float32), cs], axis=1)
    pos = jnp.arange(S_)
    groups = []
    for gi, w in enumerate(POOL_WINDOWS):
        sl = slice(gi * POOL_GROUP, (gi + 1) * POOL_GROUP)
        lo = jnp.maximum(pos + 1 - w, 0)
        cnt = jnp.minimum(pos + 1, w).astype(jnp.float32)
        mean = (cs[:, 1:, sl] - cs[:, lo, sl]) / cnt[None, :, None]
        groups.append(mean.astype(h.dtype) - pv[..., sl])
    pooled = jnp.stack(groups, axis=2)
    y_b = jnp.einsum('bsgc,gcd->bsgd', pooled, pool_w).reshape(B_, S_, B_WIDTH) * pool_scale
    return jnp.concatenate([y_a, y_b], axis=-1) @ w_out


def dsa_attention(q_lat, c_kv, iq, ik, iw, rel_bias):
    B_, S_ = c_kv.shape[:2]
    topk = min(DSA_TOPK_MAX, S_ // 4)
    n_blocks = S_ // Q_BLOCK
    scale = C_LATENT ** -0.5
    idx_scale = IDX_DIM ** -0.5
    key_pos = jnp.arange(S_)

    def block(start):
        qb = lax.dynamic_slice_in_dim(q_lat, start, Q_BLOCK, axis=1)
        iqb = lax.dynamic_slice_in_dim(iq, start, Q_BLOCK, axis=1)
        iwb = lax.dynamic_slice_in_dim(iw, start, Q_BLOCK, axis=1)
        t = start + jnp.arange(Q_BLOCK)
        rel = jax.nn.relu(jnp.einsum('bthd,bsd->bths', iqb, ik) * idx_scale)
        score = jnp.einsum('bths,bth->bts', rel, iwb).astype(jnp.float32)
        causal = key_pos[None, :] <= t[:, None]
        score = jnp.where(causal[None], score, -jnp.inf)
        _, sel = lax.top_k(score, topk)
        c_sel = jax.vmap(lambda c, i: c[i])(c_kv, sel)
        valid = sel <= t[None, :, None]
        bias = rel_bias[t5_bucket(t[None, :, None] - sel)]
        logits = (jnp.einsum('bthr,btkr->bhtk', qb, c_sel).astype(jnp.float32) * scale
                  + jnp.moveaxis(bias, -1, 1).astype(jnp.float32))
        logits = jnp.where(valid[:, None], logits, -jnp.inf)
        p = softmax32(logits, c_kv.dtype)
        return jnp.einsum('bhtk,btkr->bthr', p, c_sel)

    out = lax.map(block, jnp.arange(n_blocks) * Q_BLOCK)
    return jnp.moveaxis(out, 0, 1).reshape(B_, S_, C_HEADS, C_LATENT)


def hgrn2(q, f_logit, inp, gate, lb, norm_g):
    B_, S_, _ = q.shape
    nc = S_ // CHUNK
    f = lb + (1.0 - lb) * jax.nn.sigmoid(f_logit.astype(jnp.float32))
    log_f = jnp.log(f)
    k = 1.0 - f

    def to_chunks(a, d):
        return a.astype(jnp.float32).reshape(B_, nc, CHUNK, D_HEADS, d).transpose(1, 0, 3, 2, 4)

    qc, kc, lfc = to_chunks(q, D_KEY), to_chunks(k, D_KEY), to_chunks(log_f, D_KEY)
    ic = to_chunks(inp, D_VAL)
    causal = jnp.tril(jnp.ones((CHUNK, CHUNK), bool))

    def step(state, xs):
        qj, kj, ij, lfj = xs
        A = jnp.cumsum(lfj, axis=2)
        o_inter = jnp.einsum('bhtk,bhkv->bhtv', qj * jnp.exp(A), state)
        diff = A[:, :, :, None, :] - A[:, :, None, :, :]
        decay = jnp.exp(jnp.where(causal[:, :, None], diff, -jnp.inf))
        scores = jnp.einsum('bhtk,bhsk,bhtsk->bhts', qj, kj, decay)
        o_intra = jnp.einsum('bhts,bhsv->bhtv', scores, ij)
        A_end = A[:, :, -1]
        new_state = (jnp.exp(A_end)[..., None] * state
                     + jnp.einsum('bhsk,bhsv->bhkv', kj * jnp.exp(A_end[:, :, None] - A), ij))
        return new_state, o_inter + o_intra

    state0 = jnp.zeros((B_, D_HEADS, D_KEY, D_VAL), jnp.float32)
    _, o = lax.scan(step, state0, (qc, kc, ic, lfc))
    o = o.transpose(1, 0, 3, 2, 4).reshape(B_, S_, D_HEADS, D_VAL)
    o = o * lax.rsqrt(jnp.mean(o * o, axis=-1, keepdims=True) + EPS) * norm_g.astype(jnp.float32).reshape(D_HEADS, D_VAL)
    return (o.reshape(B_, S_, D_HEADS * D_VAL) * jax.nn.silu(gate.astype(jnp.float32))).astype(q.dtype)


def sparse_attn_hgrn_mixer(h, w_in, kv_norm, w_uv, hg_norm, w_out, rel_bias, lb):
    B_, S_, _ = h.shape
    q_lat, c_kv, iq, ik, iw, hq, hf, hi, hg = _split(h @ w_in, ODD_SPLITS)
    c_kv = rmsnorm(c_kv, kv_norm)
    lat = dsa_attention(q_lat.reshape(B_, S_, C_HEADS, C_LATENT), c_kv,
                        iq.reshape(B_, S_, IDX_HEADS, IDX_DIM), ik, iw * IDX_HEADS ** -0.5, rel_bias)
    y_c = jnp.einsum('bshr,hrd->bshd', lat, w_uv).reshape(B_, S_, C_HEADS * C_HEAD_DIM)
    y_d = hgrn2(hq, hf, hi, hg, lb, hg_norm)
    return jnp.concatenate([y_c, y_d], axis=-1) @ w_out


def memory_cross_attn(h, mem_n, wq, wkv, wo):
    B_, S_, _ = h.shape
    q = (h @ wq).reshape(B_, S_, X_HEADS, X_HEAD_DIM)
    k, v = jnp.split(mem_n @ wkv, 2, axis=-1)
    k = k.reshape(B_, -1, X_HEADS, X_HEAD_DIM)
    v = v.reshape(B_, -1, X_HEADS, X_HEAD_DIM)
    logits = jnp.einsum('bshd,bmhd->bhsm', q, k).astype(jnp.float32) * X_HEAD_DIM ** -0.5
    p = softmax32(logits, h.dtype)
    o = jnp.einsum('bhsm,bmhd->bshd', p, v).reshape(B_, S_, D_MODEL)
    return o @ wo


def peer_ffn(h, wq, sub_keys, u, v):
    B_, S_, D_ = h.shape
    T = B_ * S_
    ht = h.reshape(T, D_)

    def block(start):
        xb = lax.dynamic_slice_in_dim(ht, start, PEER_BLOCK, axis=0)
        q = (xb @ wq).reshape(PEER_BLOCK, PEER_HEADS, 2, PEER_HALF)
        s = jnp.einsum('thpd,hpnd->thpn', q, sub_keys).astype(jnp.float32)
        s1, i1 = lax.top_k(s[:, :, 0], PEER_TOPK)
        s2, i2 = lax.top_k(s[:, :, 1], PEER_TOPK)
        cand = (s1[..., :, None] + s2[..., None, :]).reshape(PEER_BLOCK, PEER_HEADS, PEER_TOPK * PEER_TOPK)
        cand_id = (i1[..., :, None] * N_KEYS + i2[..., None, :]).reshape(PEER_BLOCK, PEER_HEADS, PEER_TOPK * PEER_TOPK)
        top, pos = lax.top_k(cand, PEER_TOPK)
        eid = jnp.take_along_axis(cand_id, pos, axis=-1).reshape(PEER_BLOCK, PEER_HEADS * PEER_TOPK)
        g = jax.nn.softmax(top, axis=-1).reshape(PEER_BLOCK, PEER_HEADS * PEER_TOPK).astype(xb.dtype)
        act = jax.nn.gelu(jnp.einsum('td,ted->te', xb, u[eid]))
        return jnp.einsum('te,ted->td', g * act, v[eid])

    out = lax.map(block, jnp.arange(T // PEER_BLOCK) * PEER_BLOCK)
    return out.reshape(B_, S_, D_)


def setup_inputs(seed: int = 0) -> dict:
    key = jax.random.key(seed)
    ks = iter(jax.random.split(key, 32))
    n_even = (DEPTH + 1) // 2
    n_odd = DEPTH // 2

    def nrm(shape, scale):
        return jax.random.normal(next(ks), shape, jnp.float32) * scale

    def gain(shape):
        return 1.0 + 0.02 * jax.random.normal(next(ks), shape, jnp.float32)

    return {
        'x': nrm((BATCH, SEQ, D_MODEL), 1.0),
        'mem': nrm((BATCH, N_MEM, D_MODEL), 1.0),
        'mix_norm': gain((DEPTH, D_MODEL)),
        'even_w_in': nrm((n_even, D_MODEL, EVEN_IN), D_MODEL ** -0.5),
        'even_conv_w': nrm((n_even, CONV_WIDTH, 1, A_WIDTH), CONV_WIDTH ** -0.5),
        'even_pool_w': nrm((n_even, len(POOL_WINDOWS), POOL_GROUP, POOL_GROUP), POOL_GROUP ** -0.5),
        'even_pool_scale': gain((n_even, B_WIDTH)),
        'even_w_out': nrm((n_even, MIX_WIDTH, D_MODEL), MIX_WIDTH ** -0.5),
        'odd_w_in': nrm((n_odd, D_MODEL, ODD_IN), D_MODEL ** -0.5),
        'odd_kv_norm': gain((n_odd, C_LATENT)),
        'odd_w_uv': nrm((n_odd, C_HEADS, C_LATENT, C_HEAD_DIM), C_LATENT ** -0.5),
        'odd_hg_norm': gain((n_odd, D_HEADS * D_VAL)),
        'odd_w_out': nrm((n_odd, MIX_WIDTH, D_MODEL), MIX_WIDTH ** -0.5),
        'hgrn_gamma': nrm((DEPTH, D_HEADS * D_KEY), 0.5),
        'rel_bias': nrm((REL_BUCKETS, C_HEADS), 0.5),
        'mem_norm': gain((D_MODEL,)),
        'xattn_norm': gain((DEPTH, D_MODEL)),
        'xattn_wq': nrm((DEPTH, D_MODEL, D_MODEL), D_MODEL ** -0.5),
        'xattn_wkv': nrm((DEPTH, D_MODEL, 2 * D_MODEL), D_MODEL ** -0.5),
        'xattn_wo': nrm((DEPTH, D_MODEL, D_MODEL), D_MODEL ** -0.5),
        'peer_norm': gain((DEPTH, D_MODEL)),
        'peer_wq': nrm((DEPTH, D_MODEL, PEER_HEADS * PEER_QDIM), D_MODEL ** -0.5),
        'peer_keys': nrm((DEPTH, PEER_HEADS, 2, N_KEYS, PEER_HALF), PEER_HALF ** -0.5),
        'peer_u': nrm((DEPTH, N_EXPERTS, D_MODEL), D_MODEL ** -0.5),
        'peer_v': nrm((DEPTH, N_EXPERTS, D_MODEL), (PEER_HEADS * PEER_TOPK) ** -0.5),
        'final_norm': gain((D_MODEL,)),
    }


def reference(x, mem, mix_norm, even_w_in, even_conv_w, even_pool_w, even_pool_scale, even_w_out,
              odd_w_in, odd_kv_norm, odd_w_uv, odd_hg_norm, odd_w_out, hgrn_gamma, rel_bias, mem_norm,
              xattn_norm, xattn_wq, xattn_wkv, xattn_wo, peer_norm, peer_wq, peer_keys, peer_u, peer_v,
              final_norm):
    mem_n = rmsnorm(mem, mem_norm)
    p = jax.nn.softmax(hgrn_gamma.astype(jnp.float32), axis=0)
    lower_bounds = jnp.cumsum(p, axis=0) - p
    h = x
    for l in range(DEPTH):
        j = l // 2
        hn = rmsnorm(h, mix_norm[l])
        if l % 2 == 0:
            h = h + short_conv_pool_mixer(hn, even_w_in[j], even_conv_w[j], even_pool_w[j],
                                          even_pool_scale[j], even_w_out[j])
        else:
            h = h + sparse_attn_hgrn_mixer(hn, odd_w_in[j], odd_kv_norm[j], odd_w_uv[j], odd_hg_norm[j],
                                           odd_w_out[j], rel_bias, lower_bounds[l])
        h = h + memory_cross_attn(rmsnorm(h, xattn_norm[l]), mem_n, xattn_wq[l], xattn_wkv[l], xattn_wo[l])
        h = h + peer_ffn(rmsnorm(h, peer_norm[l]), peer_wq[l], peer_keys[l], peer_u[l], peer_v[l])
    return rmsnorm(h, final_norm)
```

```python
import functools
import math

import numpy as np
import jax
import jax.numpy as jnp
from jax import lax
from jax.experimental import pallas as pl
from jax.experimental.pallas import tpu as pltpu

F32 = jnp.float32
BF16 = jnp.bfloat16
I32 = jnp.int32

EPS = 1e-6
LANES = 128
INT_MIN = -(2 ** 31)
VMEM_LIMIT = 56 * 1024 * 1024

POOL_WINDOWS = (2, 4, 8, 16)
HALO = 16
A_WIDTH = 512
POOL_GROUP = 128

C_HEADS = 8
C_LATENT = 128
C_HEAD_DIM = 64
IDX_HEADS = 8
IDX_DIM = 64
DSA_TOPK_MAX = 256
Q_BLOCK = 128
REL_BUCKETS = 32
REL_MAX_DIST = 128

D_HEADS = 4
D_KEY = 128
HG_CHUNK = 128
HG_SUB = 16

X_HEADS = 4
X_HEAD_DIM = 256

N_KEYS = 128
PEER_HEADS = 8
PEER_TOPK = 16
NO_RANK = 127

ODD_COLS = 4096
ODD_SRC = dict(q_lat=(0, 1024), c_kv=(1024, 128), iq=(1152, 512), ik=(1664, 64), iw=(1728, 8),
               hq=(1736, 512), hf=(2248, 512), hi=(2760, 512), hg=(3272, 512))
ODD_DST = dict(q_lat=0, iq=1024, c_kv=1536, ik=1664, iw=1792, hq=2048, hf=2560, hi=3072, hg=3584)

NT_DIMS = (((1,), (1,)), ((), ()))


def _params(*sem):
    return pltpu.CompilerParams(dimension_semantics=sem, vmem_limit_bytes=VMEM_LIMIT)


def _rms(x, g):
    return x * lax.rsqrt(jnp.mean(x * x, axis=-1, keepdims=True) + EPS) * g


def _dot(a, b):
    return jnp.dot(a, b, preferred_element_type=F32)


def _dot_nt(a, b):
    return lax.dot_general(a, b, NT_DIMS, preferred_element_type=F32)


def _norm_matmul_kernel(x_ref, g_ref, w_ref, o_ref):
    y = _rms(x_ref[...], g_ref[...])
    o_ref[...] = _dot(y.astype(BF16), w_ref[...]).astype(o_ref.dtype)


def norm_matmul(x, g, w, out_dtype, tm=256):
    m, d = x.shape
    n = w.shape[1]
    return pl.pallas_call(
        _norm_matmul_kernel,
        grid=(m // tm,),
        in_specs=[pl.BlockSpec((tm, d), lambda i: (i, 0)),
                  pl.BlockSpec((1, d), lambda i: (0, 0)),
                  pl.BlockSpec((d, n), lambda i: (0, 0))],
        out_specs=pl.BlockSpec((tm, n), lambda i: (i, 0)),
        out_shape=jax.ShapeDtypeStruct((m, n), out_dtype),
        compiler_params=_params("parallel"),
        name="norm_matmul",
    )(x, g.reshape(1, d), w)


def _even_kernel(x_ref, g_ref, win_ref, cw_ref, pw_ref, ps_ref, wout_ref, o_ref, vbuf, pbuf, *, ts):
    j = pl.program_id(1)

    @pl.when(j == 0)
    def _():
        vbuf[0:HALO, :] = jnp.zeros((HALO, A_WIDTH), F32)
        pbuf[0:HALO, :] = jnp.zeros((HALO, A_WIDTH), F32)

    x = x_ref[0]
    proj = _dot(_rms(x, g_ref[...]).astype(BF16), win_ref[...])
    u = proj[:, 0:512]
    gate_c = proj[:, 512:1024]
    gate_b = proj[:, 1024:1536]
    pv = proj[:, 1536:2048]
    vbuf[HALO:HALO + ts, :] = gate_c * u
    pbuf[HALO:HALO + ts, :] = pv

    cw = cw_ref[...]
    z = (cw[2:3] * vbuf[HALO:HALO + ts, :] + cw[1:2] * vbuf[HALO - 1:HALO - 1 + ts, :]
         + cw[0:1] * vbuf[HALO - 2:HALO - 2 + ts, :])
    y_a = gate_b * z

    pos = j * ts + lax.broadcasted_iota(I32, (ts, 1), 0)
    y_b = []
    for gi, w in enumerate(POOL_WINDOWS):
        c0 = gi * POOL_GROUP
        acc = pbuf[HALO:HALO + ts, c0:c0 + POOL_GROUP]
        for k in range(1, w):
            acc = acc + pbuf[HALO - k:HALO - k + ts, c0:c0 + POOL_GROUP]
        cnt = jnp.minimum(pos + 1, w).astype(F32)
        pooled = acc / cnt - pv[:, c0:c0 + POOL_GROUP]
        y_b.append(_dot(pooled.astype(BF16), pw_ref[gi]))
    y_b = jnp.concatenate(y_b, axis=-1) * ps_ref[...]

    o_ref[0] = (x + _dot(y_a.astype(BF16), wout_ref[0:A_WIDTH, :])
                + _dot(y_b.astype(BF16), wout_ref[A_WIDTH:2 * A_WIDTH, :]))

    vbuf[0:HALO, :] = vbuf[ts:ts + HALO, :]
    pbuf[0:HALO, :] = pbuf[ts:ts + HALO, :]


def even_mixer(h, g, w_in, conv_w, pool_w, pool_scale, w_out, ts=512):
    b, s, d = h.shape
    return pl.pallas_call(
        functools.partial(_even_kernel, ts=ts),
        grid=(b, s // ts),
        in_specs=[pl.BlockSpec((1, ts, d), lambda i, j: (i, j, 0)),
                  pl.BlockSpec((1, d), lambda i, j: (0, 0)),
                  pl.BlockSpec(w_in.shape, lambda i, j: (0, 0)),
                  pl.BlockSpec(conv_w.shape, lambda i, j: (0, 0)),
                  pl.BlockSpec(pool_w.shape, lambda i, j: (0, 0, 0)),
                  pl.BlockSpec(pool_scale.shape, lambda i, j: (0, 0)),
                  pl.BlockSpec(w_out.shape, lambda i, j: (0, 0))],
        out_specs=pl.BlockSpec((1, ts, d), lambda i, j: (i, j, 0)),
        out_shape=jax.ShapeDtypeStruct(h.shape, F32),
        scratch_shapes=[pltpu.VMEM((HALO + ts, A_WIDTH), F32), pltpu.VMEM((HALO + ts, A_WIDTH), F32)],
        compiler_params=_params("parallel", "arbitrary"),
        name="even_mixer",
    )(h, g.reshape(1, d), w_in, conv_w, pool_w, pool_scale, w_out)


def _t5_bucket_table():
    t = np.arange(Q_BLOCK)[:, None]
    s = np.arange(Q_BLOCK)[None, :]
    dist = np.concatenate([t - s, t - s + Q_BLOCK], axis=1)
    max_exact = REL_BUCKETS // 2
    d = np.maximum(dist, 0)
    log_ratio = (np.log(np.maximum(d, 1).astype(np.float32) / np.float32(max_exact))
                 / np.float32(math.log(REL_MAX_DIST / max_exact)))
    large = max_exact + (log_ratio * np.float32(REL_BUCKETS - max_exact)).astype(np.int32)
    large = np.minimum(large, REL_BUCKETS - 1)
    return np.where(d < max_exact, d, large).astype(np.int32)


def _dsa_kernel(rb_ref, bk_ref, q_ref, iq_ref, iw_ref, ckv_ref, ik_ref, kvn_ref, wuv_ref, o_ref,
                ckv_s, ik_s, bias_s, key_s, sel_s, lg_s, *, s_len, topk):
    qi = pl.program_id(1)
    scale = C_LATENT ** -0.5
    far_bucket = REL_BUCKETS - 1

    @pl.when(qi == 0)
    def _():
        ckv_s[...] = _rms(ckv_ref[0], kvn_ref[...]).astype(BF16)
        ik_s[...] = ik_ref[0][:, 0:IDX_DIM].astype(BF16)
        bk = bk_ref[...]
        for h in range(C_HEADS):
            tile = jnp.zeros(bk.shape, F32)
            for k in range(REL_BUCKETS):
                tile = jnp.where(bk == k, rb_ref[k, h], tile)
            bias_s[h] = tile

    t_idx = qi * Q_BLOCK + lax.broadcasted_iota(I32, (Q_BLOCK, 1), 0)
    s_idx = lax.broadcasted_iota(I32, (1, s_len), 1)
    causal = s_idx <= t_idx

    iq = iq_ref[0]
    iw = iw_ref[0] * (IDX_HEADS ** -0.5)
    ik = ik_s[...]
    score = jnp.zeros((Q_BLOCK, s_len), F32)
    for h in range(IDX_HEADS):
        s_h = _dot_nt(iq[:, h * IDX_DIM:(h + 1) * IDX_DIM].astype(BF16), ik)
        score = score + jnp.maximum(s_h * (IDX_DIM ** -0.5), 0.0) * iw[:, h:h + 1]

    bits = lax.bitcast_convert_type(score + 0.0, I32)
    key = jnp.where(bits < 0, bits ^ jnp.int32(0x7FFFFFFF), bits)
    key_s[...] = jnp.where(causal, key, jnp.int32(INT_MIN))

    def count(mask):
        return jnp.sum(jnp.where(mask, 1.0, 0.0), axis=1, keepdims=True)

    def thr_step(i, ans):
        cand = ans + lax.shift_left(jnp.int32(1), 31 - i)
        return jnp.where(count(key_s[...] >= cand) >= topk, cand, ans)

    thr = lax.fori_loop(0, 32, thr_step, jnp.full((Q_BLOCK, 1), INT_MIN, I32))
    key = key_s[...]
    gt = key > thr
    eq = key == thr
    need = topk - count(gt)
    tie_rows = jnp.logical_and(count(eq) > need, thr != jnp.int32(INT_MIN))
    any_tie = jnp.max(jnp.where(tie_rows, 1.0, 0.0)) > 0.0

    def tie_cut():
        def step(i, cut):
            cand = cut + lax.shift_left(jnp.int32(1), 11 - i)
            c = count(jnp.logical_and(key_s[...] == thr, s_idx < cand))
            return jnp.where(c <= need, cand, cut)
        return lax.fori_loop(0, 12, step, jnp.zeros((Q_BLOCK, 1), I32))

    cut = lax.cond(any_tie, tie_cut, lambda: jnp.full((Q_BLOCK, 1), s_len, I32))
    sel = jnp.logical_and(causal, jnp.logical_or(gt, jnp.logical_and(eq, s_idx < cut)))
    sel_s[...] = jnp.where(sel, 1.0, 0.0)

    off = pl.multiple_of(qi * Q_BLOCK, Q_BLOCK)
    off_prev = pl.multiple_of(jnp.maximum(qi - 1, 0) * Q_BLOCK, Q_BLOCK)
    ckv = ckv_s[...]
    lats = []
    for h in range(C_HEADS):
        qh = q_ref[0][:, h * C_LATENT:(h + 1) * C_LATENT].astype(BF16)
        lg_s[...] = _dot_nt(qh, ckv) * scale + rb_ref[far_bucket, h]
        lg_s[:, pl.ds(off, Q_BLOCK)] = (_dot_nt(qh, ckv_s[pl.ds(off, Q_BLOCK), :]) * scale
                                        + bias_s[h][:, 0:Q_BLOCK])

        @pl.when(qi > 0)
        def _():
            lg_s[:, pl.ds(off_prev, Q_BLOCK)] = (_dot_nt(qh, ckv_s[pl.ds(off_prev, Q_BLOCK), :]) * scale
                                                 + bias_s[h][:, Q_BLOCK:2 * Q_BLOCK])

        logits = jnp.where(sel_s[...] > 0.0, lg_s[...], -jnp.inf)
        m = jnp.max(logits, axis=1, keepdims=True)
        p = jnp.exp(logits - m)
        p = p / jnp.sum(p, axis=1, keepdims=True)
        lats.append(_dot(p.astype(BF16), ckv).astype(BF16))
    o_ref[0] = _dot(jnp.concatenate(lats, axis=-1), wuv_ref[...])


def dsa_attention(proj, rel_bias, kv_norm, wuv_bd):
    b, s, _ = proj.shape
    topk = min(DSA_TOPK_MAX, s // 4)
    bk = jnp.asarray(_t5_bucket_table())
    col = lambda name, width: ODD_DST[name] // width
    return pl.pallas_call(
        functools.partial(_dsa_kernel, s_len=s, topk=topk),
        grid=(b, s // Q_BLOCK),
        in_specs=[pl.BlockSpec(memory_space=pltpu.SMEM),
                  pl.BlockSpec(bk.shape, lambda i, j: (0, 0)),
                  pl.BlockSpec((1, Q_BLOCK, 1024), lambda i, j: (i, j, col("q_lat", 1024))),
                  pl.BlockSpec((1, Q_BLOCK, 512), lambda i, j: (i, j, col("iq", 512))),
                  pl.BlockSpec((1, Q_BLOCK, LANES), lambda i, j: (i, j, col("iw", LANES))),
                  pl.BlockSpec((1, s, LANES), lambda i, j: (i, 0, col("c_kv", LANES))),
                  pl.BlockSpec((1, s, LANES), lambda i, j: (i, 0, col("ik", LANES))),
                  pl.BlockSpec((1, C_LATENT), lambda i, j: (0, 0)),
                  pl.BlockSpec(wuv_bd.shape, lambda i, j: (0, 0))],
        out_specs=pl.BlockSpec((1, Q_BLOCK, C_HEADS * C_HEAD_DIM), lambda i, j: (i, j, 0)),
        out_shape=jax.ShapeDtypeStruct((b, s, C_HEADS * C_HEAD_DIM), F32),
        scratch_shapes=[pltpu.VMEM((s, C_LATENT), BF16),
                        pltpu.VMEM((s, IDX_DIM), BF16),
                        pltpu.VMEM((C_HEADS, Q_BLOCK, 2 * Q_BLOCK), F32),
                        pltpu.VMEM((Q_BLOCK, s), I32),
                        pltpu.VMEM((Q_BLOCK, s), F32),
                        pltpu.VMEM((Q_BLOCK, s), F32)],
        compiler_params=_params("parallel", "arbitrary"),
        name="dsa_attention",
    )(rel_bias, bk, proj, proj, proj, proj, proj, kv_norm.reshape(1, C_LATENT), wuv_bd)


def _hgrn_kernel(gam_ref, nrm_ref, q_ref, f_ref, i_ref, g_ref, o_ref, st_ref, *, layer):
    c = pl.program_id(1)
    ch = HG_CHUNK

    @pl.when(c == 0)
    def _():
        st_ref[...] = jnp.zeros(st_ref.shape, F32)

    gam = gam_ref[...]
    pe = jnp.exp(gam - jnp.max(gam, axis=0, keepdims=True))
    pe = pe / jnp.sum(pe, axis=0, keepdims=True)
    lb = jnp.zeros((1, gam.shape[1]), F32)
    for i in range(layer):
        lb = lb + pe[i:i + 1]

    f = lb + (1.0 - lb) * jax.nn.sigmoid(f_ref[0])
    kk = 1.0 - f
    a = jnp.log(f)
    row = lax.broadcasted_iota(I32, (ch, 1), 0)
    sh = 1
    while sh < ch:
        a = a + jnp.where(row >= sh, pltpu.roll(a, sh, axis=0), 0.0)
        sh *= 2
    q = q_ref[0]
    inp = i_ref[0]
    a_end = a[ch - 1:ch, :]
    qe = q * jnp.exp(a)
    kd_end = kk * jnp.exp(a_end - a)
    trow = lax.broadcasted_iota(I32, (HG_SUB, 1), 0)

    outs = []
    for h in range(D_HEADS):
        sl = slice(h * D_KEY, (h + 1) * D_KEY)
        ah, qh, kh, ih = a[:, sl], q[:, sl], kk[:, sl], inp[:, sl]
        ih_bf = ih.astype(BF16)
        st_t = st_ref[h]
        o = _dot_nt(qe[:, sl].astype(BF16), st_t.astype(BF16))
        blocks = []
        for i in range(ch // HG_SUB):
            r0 = i * HG_SUB
            ai, qi, ki, ii = ah[r0:r0 + HG_SUB], qh[r0:r0 + HG_SUB], kh[r0:r0 + HG_SUB], ih[r0:r0 + HG_SUB]
            acc = jnp.zeros((HG_SUB, D_KEY), F32)
            if i > 0:
                r_i = ah[r0 - 1:r0]
                qd = qi * jnp.exp(ai - r_i)
                kd = kh[0:r0] * jnp.exp(r_i - ah[0:r0])
                s_off = _dot_nt(qd.astype(BF16), kd.astype(BF16))
                acc = _dot(s_off.astype(BF16), ih_bf[0:r0])
            for s in range(HG_SUB):
                e = jnp.exp(jnp.where(trow >= s, ai - ai[s:s + 1], -jnp.inf))
                sc = jnp.sum(qi * ki[s:s + 1] * e, axis=1, keepdims=True)
                acc = acc + sc * ii[s:s + 1]
            blocks.append(acc)
        o = o + jnp.concatenate(blocks, axis=0)
        st_ref[h] = st_t * jnp.exp(a_end[:, sl]) + _dot(ih.T.astype(BF16), kd_end[:, sl].astype(BF16))
        outs.append(_rms(o, nrm_ref[:, sl]))
    o_ref[0] = jnp.concatenate(outs, axis=-1) * jax.nn.silu(g_ref[0])


def hgrn2(proj, gamma, hg_norm, layer):
    b, s, _ = proj.shape
    w = D_HEADS * D_KEY
    col = lambda name: ODD_DST[name] // w
    spec = lambda name: pl.BlockSpec((1, HG_CHUNK, w), lambda i, j: (i, j, col(name)))
    return pl.pallas_call(
        functools.partial(_hgrn_kernel, layer=layer),
        grid=(b, s // HG_CHUNK),
        in_specs=[pl.BlockSpec(gamma.shape, lambda i, j: (0, 0)),
                  pl.BlockSpec((1, w), lambda i, j: (0, 0)),
                  spec("hq"), spec("hf"), spec("hi"), spec("hg")],
        out_specs=pl.BlockSpec((1, HG_CHUNK, w), lambda i, j: (i, j, 0)),
        out_shape=jax.ShapeDtypeStruct((b, s, w), F32),
        scratch_shapes=[pltpu.VMEM((D_HEADS, D_KEY, D_KEY), F32)],
        compiler_params=_params("parallel", "arbitrary"),
        name="hgrn2",
    )(gamma, hg_norm.reshape(1, w), proj, proj, proj, proj)


def _outproj_kernel(h_ref, a_ref, b_ref, w_ref, o_ref):
    ka = a_ref.shape[1]
    o_ref[...] = (h_ref[...] + _dot(a_ref[...].astype(BF16), w_ref[0:ka, :])
                  + _dot(b_ref[...].astype(BF16), w_ref[ka:, :]))


def outproj_residual(h, a, b, w, tm=512):
    m, d = h.shape
    return pl.pallas_call(
        _outproj_kernel,
        grid=(m // tm,),
        in_specs=[pl.BlockSpec((tm, d), lambda i: (i, 0)),
                  pl.BlockSpec((tm, a.shape[1]), lambda i: (i, 0)),
                  pl.BlockSpec((tm, b.shape[1]), lambda i: (i, 0)),
                  pl.BlockSpec(w.shape, lambda i: (0, 0))],
        out_specs=pl.BlockSpec((tm, d), lambda i: (i, 0)),
        out_shape=jax.ShapeDtypeStruct((m, d), F32),
        compiler_params=_params("parallel"),
        name="outproj_residual",
    )(h, a, b, w)


def _xattn_kernel(h_ref, g_ref, wq_ref, kv_ref, wo_ref, o_ref):
    x = h_ref[0]
    q = _dot(_rms(x, g_ref[...]).astype(BF16), wq_ref[...])
    kv = kv_ref[0]
    dm = X_HEADS * X_HEAD_DIM
    outs = []
    for h in range(X_HEADS):
        sl = slice(h * X_HEAD_DIM, (h + 1) * X_HEAD_DIM)
        logits = _dot_nt(q[:, sl].astype(BF16), kv[:, sl]) * (X_HEAD_DIM ** -0.5)
        p = jnp.exp(logits - jnp.max(logits, axis=1, keepdims=True))
        p = p / jnp.sum(p, axis=1, keepdims=True)
        outs.append(_dot(p.astype(BF16), kv[:, dm + h * X_HEAD_DIM:dm + (h + 1) * X_HEAD_DIM]).astype(BF16))
    o_ref[0] = x + _dot(jnp.concatenate(outs, axis=-1), wo_ref[...])


def memory_cross_attn(h, g, wq, kv, wo, ts=512):
    b, s, d = h.shape
    return pl.pallas_call(
        _xattn_kernel,
        grid=(b, s // ts),
        in_specs=[pl.BlockSpec((1, ts, d), lambda i, j: (i, j, 0)),
                  pl.BlockSpec((1, d), lambda i, j: (0, 0)),
                  pl.BlockSpec(wq.shape, lambda i, j: (0, 0)),
                  pl.BlockSpec((1,) + kv.shape[1:], lambda i, j: (i, 0, 0)),
                  pl.BlockSpec(wo.shape, lambda i, j: (0, 0))],
        out_specs=pl.BlockSpec((1, ts, d), lambda i, j: (i, j, 0)),
        out_shape=jax.ShapeDtypeStruct(h.shape, F32),
        compiler_params=_params("parallel", "parallel"),
        name="memory_cross_attn",
    )(h, g.reshape(1, d), wq, kv, wo)


def _extract_top(work, iota, rounds):
    rows = work.shape[0]
    rank = jnp.full(work.shape, NO_RANK, I32)
    vals = jnp.zeros((rounds, work.shape[1]), F32)
    riota = lax.broadcasted_iota(I32, (rounds, 1), 0)
    for r in range(rounds):
        m = jnp.max(work, axis=0, keepdims=True)
        first = jnp.min(jnp.where(work == m, iota, rows), axis=0, keepdims=True)
        hit = iota == first
        rank = jnp.where(hit, r, rank)
        work = jnp.where(hit, -jnp.inf, work)
        vals = jnp.where(riota == r, m, vals)
    return vals, rank


def _route_kernel(h_ref, g_ref, wq_ref, keys_ref, hn_ref, a_ref, n1_ref, b_ref, r2_ref, *, tt):
    hn = _rms(h_ref[...], g_ref[...]).astype(BF16)
    hn_ref[...] = hn
    q = _dot(hn, wq_ref[...])
    k = PEER_TOPK
    iota_keys = lax.broadcasted_iota(I32, (N_KEYS, tt), 0)
    iota8 = lax.broadcasted_iota(I32, (8, 1), 0)
    n_full = 8
    cand_rows = k + 8 * (n_full - 1) + 8
    iota_cand = lax.broadcasted_iota(I32, (cand_rows, tt), 0)
    for h in range(PEER_HEADS):
        s1 = _dot_nt(keys_ref[h, 0], q[:, (2 * h) * N_KEYS:(2 * h + 1) * N_KEYS].astype(BF16))
        s2 = _dot_nt(keys_ref[h, 1], q[:, (2 * h + 1) * N_KEYS:(2 * h + 2) * N_KEYS].astype(BF16))
        a, rank1 = _extract_top(s1, iota_keys, k)
        b, rank2 = _extract_top(s2, iota_keys, k)
        blocks = [a[0:1] + b]
        for i in range(1, n_full):
            blocks.append(jnp.where(iota8 < k // (i + 1), a[i:i + 1] + b[0:8], -jnp.inf))
        blocks.append(a[8:16] + b[0:1])
        cand = jnp.concatenate(blocks, axis=0)
        _, crank = _extract_top(cand, iota_cand, k)
        chosen = crank < k
        z = jnp.sum(jnp.where(chosen, jnp.exp(cand - (a[0:1] + b[0:1])), 0.0), axis=0, keepdims=True)
        chosen_f = jnp.where(chosen, 1.0, 0.0)
        n_lo = jnp.zeros((8, tt), F32)
        n_lo = jnp.where(iota8 == 0, jnp.sum(chosen_f[0:k], axis=0, keepdims=True), n_lo)
        for i in range(1, n_full):
            r0 = k + 8 * (i - 1)
            n_lo = jnp.where(iota8 == i, jnp.sum(chosen_f[r0:r0 + 8], axis=0, keepdims=True), n_lo)
        n_hi = chosen_f[cand_rows - 8:cand_rows]
        n1 = jnp.zeros((N_KEYS, tt), F32)
        for r in range(k):
            n_r = n_lo[r:r + 1] if r < 8 else n_hi[r - 8:r - 7]
            n1 = jnp.where(rank1 == r, n_r, n1)
        a_ref[h] = jnp.exp(s1 - a[0:1]) / z
        n1_ref[h] = n1
        b_ref[h] = jnp.exp(s2 - b[0:1])
        r2_ref[h] = rank2.astype(F32)


def peer_route(h2d, g, wq, keys, tt=256):
    t, d = h2d.shape
    tab = jax.ShapeDtypeStruct((PEER_HEADS, N_KEYS, t), F32)
    tab_spec = pl.BlockSpec((PEER_HEADS, N_KEYS, tt), lambda i: (0, 0, i))
    return pl.pallas_call(
        functools.partial(_route_kernel, tt=tt),
        grid=(t // tt,),
        in_specs=[pl.BlockSpec((tt, d), lambda i: (i, 0)),
                  pl.BlockSpec((1, d), lambda i: (0, 0)),
                  pl.BlockSpec(wq.shape, lambda i: (0, 0)),
                  pl.BlockSpec(keys.shape, lambda i: (0, 0, 0, 0))],
        out_specs=[pl.BlockSpec((tt, d), lambda i: (i, 0)), tab_spec, tab_spec, tab_spec, tab_spec],
        out_shape=[jax.ShapeDtypeStruct((t, d), BF16), tab, tab, tab, tab],
        compiler_params=_params("parallel"),
        name="peer_route",
    )(h2d, g.reshape(1, d), wq, keys)


def _peer_dense_kernel(h_ref, hn_ref, a_ref, n1_ref, b_ref, r2_ref, u_ref, vt_ref, fg_ref, o_ref, acc_ref,
                       *, te, final_norm):
    e = pl.program_id(1)

    @pl.when(e == 0)
    def _():
        acc_ref[...] = jnp.zeros(acc_ref.shape, F32)

    act = jax.nn.gelu(_dot_nt(u_ref[...], hn_ref[...]))
    nb = te // N_KEYS
    parts = []
    for kb in range(nb):
        i1 = e * nb + kb
        gate = jnp.zeros((N_KEYS, act.shape[1]), F32)
        for h in range(PEER_HEADS):
            a_row = a_ref[h, pl.ds(i1, 1), :]
            n_row = n1_ref[h, pl.ds(i1, 1), :]
            gate = gate + jnp.where(r2_ref[h] < n_row, b_ref[h] * a_row, 0.0)
        parts.append((gate * act[kb * N_KEYS:(kb + 1) * N_KEYS]).astype(BF16))
    acc_ref[...] += _dot(vt_ref[...], jnp.concatenate(parts, axis=0))

    @pl.when(e == pl.num_programs(1) - 1)
    def _():
        out = h_ref[...] + acc_ref[...].T
        if final_norm:
            out = _rms(out, fg_ref[...])
        o_ref[...] = out


def peer_dense(h2d, hn, a, n1, b, r2, u, vt, final_g, final_norm, tt=512, te=512):
    t, d = h2d.shape
    ne = u.shape[0]
    tab_spec = pl.BlockSpec((PEER_HEADS, N_KEYS, tt), lambda i, j: (0, 0, i))
    return pl.pallas_call(
        functools.partial(_peer_dense_kernel, te=te, final_norm=final_norm),
        grid=(t // tt, ne // te),
        in_specs=[pl.BlockSpec((tt, d), lambda i, j: (i, 0)),
                  pl.BlockSpec((tt, d), lambda i, j: (i, 0)),
                  tab_spec, tab_spec, tab_spec, tab_spec,
                  pl.BlockSpec((te, d), lambda i, j: (j, 0)),
                  pl.BlockSpec((d, te), lambda i, j: (0, j)),
                  pl.BlockSpec((1, d), lambda i, j: (0, 0))],
        out_specs=pl.BlockSpec((tt, d), lambda i, j: (i, 0)),
        out_shape=jax.ShapeDtypeStruct((t, d), F32),
        scratch_shapes=[pltpu.VMEM((d, tt), F32)],
        compiler_params=_params("parallel", "arbitrary"),
        name="peer_dense",
    )(h2d, hn, a, n1, b, r2, u, vt, final_g.reshape(1, d))


def _pad_odd_w_in(w):
    out = jnp.zeros((w.shape[0], ODD_COLS), w.dtype)
    for name, (src, width) in ODD_SRC.items():
        out = lax.dynamic_update_slice(out, w[:, src:src + width], (0, ODD_DST[name]))
    return out


def _block_diag_uv(w_uv):
    out = jnp.zeros((C_HEADS * C_LATENT, C_HEADS * C_HEAD_DIM), w_uv.dtype)
    for h in range(C_HEADS):
        out = lax.dynamic_update_slice(out, w_uv[h], (h * C_LATENT, h * C_HEAD_DIM))
    return out


def kernel(x, mem, mix_norm, even_w_in, even_conv_w, even_pool_w, even_pool_scale, even_w_out, odd_w_in, odd_kv_norm, odd_w_uv, odd_hg_norm, odd_w_out, hgrn_gamma, rel_bias, mem_norm, xattn_norm, xattn_wq, xattn_wkv, xattn_wo, peer_norm, peer_wq, peer_keys, peer_u, peer_v, final_norm):
    bsz, s, d = x.shape
    t = bsz * s
    depth = mix_norm.shape[0]
    mem2d = mem.reshape(-1, d)
    h = x
    for l in range(depth):
        j = l // 2
        if l % 2 == 0:
            h = even_mixer(h, mix_norm[l], even_w_in[j].astype(BF16), even_conv_w[j].reshape(3, A_WIDTH),
                           even_pool_w[j].astype(BF16), even_pool_scale[j].reshape(1, -1),
                           even_w_out[j].astype(BF16))
        else:
            proj = norm_matmul(h.reshape(t, d), mix_norm[l], _pad_odd_w_in(odd_w_in[j]).astype(BF16), F32)
            proj = proj.reshape(bsz, s, ODD_COLS)
            y_c = dsa_attention(proj, rel_bias, odd_kv_norm[j], _block_diag_uv(odd_w_uv[j]).astype(BF16))
            y_d = hgrn2(proj, hgrn_gamma, odd_hg_norm[j], l)
            h = outproj_residual(h.reshape(t, d), y_c.reshape(t, -1), y_d.reshape(t, -1),
                                 odd_w_out[j].astype(BF16)).reshape(bsz, s, d)
        kv = norm_matmul(mem2d, mem_norm, xattn_wkv[l].astype(BF16), BF16).reshape(bsz, -1, 2 * d)
        h = memory_cross_attn(h, xattn_norm[l], xattn_wq[l].astype(BF16), kv, xattn_wo[l].astype(BF16))
        h2d = h.reshape(t, d)
        hn, a, n1, b, r2 = peer_route(h2d, peer_norm[l], peer_wq[l].astype(BF16), peer_keys[l].astype(BF16))
        h = peer_dense(h2d, hn, a, n1, b, r2, peer_u[l].astype(BF16), peer_v[l].T.astype(BF16),
                       final_norm, l == depth - 1).reshape(bsz, s, d)
    return h
```

```python
import functools
import math

import numpy as np
import jax
import jax.numpy as jnp
from jax import lax
from jax.experimental import pallas as pl
from jax.experimental.pallas import tpu as pltpu

F32 = jnp.float32
BF16 = jnp.bfloat16
I32 = jnp.int32

EPS = 1e-6
LANES = 128
INT_MIN = -(2 ** 31)
VMEM_LIMIT = 56 * 1024 * 1024

POOL_WINDOWS = (2, 4, 8, 16)
HALO = 16
A_WIDTH = 512
POOL_GROUP = 128

C_HEADS = 8
C_LATENT = 128
C_HEAD_DIM = 64
IDX_HEADS = 8
IDX_DIM = 64
DSA_TOPK_MAX = 256
Q_BLOCK = 128
DSA_TIERS = 4
REL_BUCKETS = 32
REL_MAX_DIST = 128

D_HEADS = 4
D_KEY = 128
HG_CHUNK = 128
HG_SUB = 16

X_HEADS = 4
X_HEAD_DIM = 256

N_KEYS = 128
PEER_HEADS = 8
PEER_TOPK = 16
NO_RANK = 127
PEER_SUB = 256
PEER_COLS = 256

ODD_COLS = 4096
ODD_SRC = dict(q_lat=(0, 1024), c_kv=(1024, 128), iq=(1152, 512), ik=(1664, 64), iw=(1728, 8),
               hq=(1736, 512), hf=(2248, 512), hi=(2760, 512), hg=(3272, 512))
ODD_DST = dict(q_lat=0, iq=1024, c_kv=1536, ik=1664, iw=1792, hq=2048, hf=2560, hi=3072, hg=3584)

NT_DIMS = (((1,), (1,)), ((), ()))


def _params(*sem):
    return pltpu.CompilerParams(dimension_semantics=sem, vmem_limit_bytes=VMEM_LIMIT)


def _rms(x, g):
    return x * lax.rsqrt(jnp.mean(x * x, axis=-1, keepdims=True) + EPS) * g


def _dot(a, b):
    return jnp.dot(a, b, preferred_element_type=F32)


def _dot_nt(a, b):
    return lax.dot_general(a, b, NT_DIMS, preferred_element_type=F32)


def _norm_matmul_kernel(x_ref, g_ref, w_ref, o_ref):
    y = _rms(x_ref[...], g_ref[...])
    o_ref[...] = _dot(y.astype(BF16), w_ref[...]).astype(o_ref.dtype)


def norm_matmul(x, g, w, out_dtype, tm=256):
    m, d = x.shape
    n = w.shape[1]
    return pl.pallas_call(
        _norm_matmul_kernel,
        grid=(m // tm,),
        in_specs=[pl.BlockSpec((tm, d), lambda i: (i, 0)),
                  pl.BlockSpec((1, d), lambda i: (0, 0)),
                  pl.BlockSpec((d, n), lambda i: (0, 0))],
        out_specs=pl.BlockSpec((tm, n), lambda i: (i, 0)),
        out_shape=jax.ShapeDtypeStruct((m, n), out_dtype),
        compiler_params=_params("parallel"),
        name="norm_matmul",
    )(x, g.reshape(1, d), w)


def _even_kernel(x_ref, g_ref, win_ref, cw_ref, pw_ref, ps_ref, wout_ref, o_ref, vbuf, pbuf, *, ts):
    j = pl.program_id(1)

    @pl.when(j == 0)
    def _():
        vbuf[0:HALO, :] = jnp.zeros((HALO, A_WIDTH), F32)
        pbuf[0:HALO, :] = jnp.zeros((HALO, A_WIDTH), F32)

    x = x_ref[0]
    proj = _dot(_rms(x, g_ref[...]).astype(BF16), win_ref[...])
    u = proj[:, 0:512]
    gate_c = proj[:, 512:1024]
    gate_b = proj[:, 1024:1536]
    pv = proj[:, 1536:2048]
    vbuf[HALO:HALO + ts, :] = gate_c * u
    pbuf[HALO:HALO + ts, :] = pv

    cw = cw_ref[...]
    z = (cw[2:3] * vbuf[HALO:HALO + ts, :] + cw[1:2] * vbuf[HALO - 1:HALO - 1 + ts, :]
         + cw[0:1] * vbuf[HALO - 2:HALO - 2 + ts, :])
    y_a = gate_b * z

    pos = j * ts + lax.broadcasted_iota(I32, (ts, 1), 0)
    y_b = []
    for gi, w in enumerate(POOL_WINDOWS):
        c0 = gi * POOL_GROUP
        acc = pbuf[HALO:HALO + ts, c0:c0 + POOL_GROUP]
        for k in range(1, w):
            acc = acc + pbuf[HALO - k:HALO - k + ts, c0:c0 + POOL_GROUP]
        cnt = jnp.minimum(pos + 1, w).astype(F32)
        pooled = acc / cnt - pv[:, c0:c0 + POOL_GROUP]
        y_b.append(_dot(pooled.astype(BF16), pw_ref[gi]))
    y_b = jnp.concatenate(y_b, axis=-1) * ps_ref[...]

    o_ref[0] = (x + _dot(y_a.astype(BF16), wout_ref[0:A_WIDTH, :])
                + _dot(y_b.astype(BF16), wout_ref[A_WIDTH:2 * A_WIDTH, :]))

    vbuf[0:HALO, :] = vbuf[ts:ts + HALO, :]
    pbuf[0:HALO, :] = pbuf[ts:ts + HALO, :]


def even_mixer(h, g, w_in, conv_w, pool_w, pool_scale, w_out, ts=512):
    b, s, d = h.shape
    return pl.pallas_call(
        functools.partial(_even_kernel, ts=ts),
        grid=(b, s // ts),
        in_specs=[pl.BlockSpec((1, ts, d), lambda i, j: (i, j, 0)),
                  pl.BlockSpec((1, d), lambda i, j: (0, 0)),
                  pl.BlockSpec(w_in.shape, lambda i, j: (0, 0)),
                  pl.BlockSpec(conv_w.shape, lambda i, j: (0, 0)),
                  pl.BlockSpec(pool_w.shape, lambda i, j: (0, 0, 0)),
                  pl.BlockSpec(pool_scale.shape, lambda i, j: (0, 0)),
                  pl.BlockSpec(w_out.shape, lambda i, j: (0, 0))],
        out_specs=pl.BlockSpec((1, ts, d), lambda i, j: (i, j, 0)),
        out_shape=jax.ShapeDtypeStruct(h.shape, F32),
        scratch_shapes=[pltpu.VMEM((HALO + ts, A_WIDTH), F32), pltpu.VMEM((HALO + ts, A_WIDTH), F32)],
        compiler_params=_params("parallel", "arbitrary"),
        name="even_mixer",
    )(h, g.reshape(1, d), w_in, conv_w, pool_w, pool_scale, w_out)


def _t5_bucket_table():
    t = np.arange(Q_BLOCK)[:, None]
    s = np.arange(Q_BLOCK)[None, :]
    dist = np.concatenate([t - s, t - s + Q_BLOCK], axis=1)
    max_exact = REL_BUCKETS // 2
    d = np.maximum(dist, 0)
    log_ratio = (np.log(np.maximum(d, 1).astype(np.float32) / np.float32(max_exact))
                 / np.float32(math.log(REL_MAX_DIST / max_exact)))
    large = max_exact + (log_ratio * np.float32(REL_BUCKETS - max_exact)).astype(np.int32)
    large = np.minimum(large, REL_BUCKETS - 1)
    return np.where(d < max_exact, d, large).astype(np.int32)


def _dsa_kernel(rb_ref, bk_ref, q_ref, iq_ref, iw_ref, ckv_ref, ik_ref, kvn_ref, wuv_ref, o_ref,
                ckv_s, ik_s, bias_s, key_s, sel_s, lg_s, *, s_len, topk):
    qi = pl.program_id(1)

    @pl.when(qi == 0)
    def _():
        ckv_s[...] = _rms(ckv_ref[0], kvn_ref[...]).astype(BF16)
        ik_s[...] = ik_ref[0][:, 0:IDX_DIM].astype(BF16)
        bk = bk_ref[...]
        for h in range(C_HEADS):
            tile = jnp.zeros(bk.shape, F32)
            for k in range(REL_BUCKETS):
                tile = jnp.where(bk == k, rb_ref[k, h], tile)
            bias_s[h] = tile

    n_q = s_len // Q_BLOCK
    tiers = DSA_TIERS if n_q % DSA_TIERS == 0 else 1
    for tier in range(tiers):
        lo, hi = tier * n_q // tiers, (tier + 1) * n_q // tiers

        @pl.when(jnp.logical_and(qi >= lo, qi < hi))
        def _(s_eff=hi * Q_BLOCK):
            _dsa_block(rb_ref, q_ref, iq_ref, iw_ref, wuv_ref, o_ref, ckv_s, ik_s, bias_s, key_s, sel_s, lg_s,
                       qi=qi, s_eff=s_eff, topk=topk)


def _dsa_block(rb_ref, q_ref, iq_ref, iw_ref, wuv_ref, o_ref, ckv_s, ik_s, bias_s, key_s, sel_s, lg_s, *, qi, s_eff, topk):
    scale = C_LATENT ** -0.5
    far_bucket = REL_BUCKETS - 1
    t_idx = qi * Q_BLOCK + lax.broadcasted_iota(I32, (Q_BLOCK, 1), 0)
    s_idx = lax.broadcasted_iota(I32, (1, s_eff), 1)
    causal = s_idx <= t_idx
    keys = slice(0, s_eff)

    iq = iq_ref[0]
    iw = iw_ref[0] * (IDX_HEADS ** -0.5)
    ik = ik_s[keys, :]
    score = jnp.zeros((Q_BLOCK, s_eff), F32)
    for h in range(IDX_HEADS):
        s_h = _dot_nt(iq[:, h * IDX_DIM:(h + 1) * IDX_DIM].astype(BF16), ik)
        score = score + jnp.maximum(s_h * (IDX_DIM ** -0.5), 0.0) * iw[:, h:h + 1]

    bits = lax.bitcast_convert_type(score + 0.0, I32)
    key = jnp.where(bits < 0, bits ^ jnp.int32(0x7FFFFFFF), bits)
    key_s[:, keys] = jnp.where(causal, key, jnp.int32(INT_MIN))

    def count(mask):
        return jnp.sum(jnp.where(mask, 1.0, 0.0), axis=1, keepdims=True)

    def thr_step(i, ans):
        cand = ans + lax.shift_left(jnp.int32(1), 31 - i)
        return jnp.where(count(key_s[:, keys] >= cand) >= topk, cand, ans)

    thr = lax.fori_loop(0, 32, thr_step, jnp.full((Q_BLOCK, 1), INT_MIN, I32))
    key = key_s[:, keys]
    gt = key > thr
    eq = key == thr
    need = topk - count(gt)
    tie_rows = jnp.logical_and(count(eq) > need, thr != jnp.int32(INT_MIN))
    any_tie = jnp.max(jnp.where(tie_rows, 1.0, 0.0)) > 0.0
    cut_bits = s_eff.bit_length()

    def tie_cut():
        def step(i, cut):
            cand = cut + lax.shift_left(jnp.int32(1), cut_bits - 1 - i)
            c = count(jnp.logical_and(key_s[:, keys] == thr, s_idx < cand))
            return jnp.where(c <= need, cand, cut)
        return lax.fori_loop(0, cut_bits, step, jnp.zeros((Q_BLOCK, 1), I32))

    cut = lax.cond(any_tie, tie_cut, lambda: jnp.full((Q_BLOCK, 1), s_eff, I32))
    sel = jnp.logical_and(causal, jnp.logical_or(gt, jnp.logical_and(eq, s_idx < cut)))
    sel_s[:, keys] = jnp.where(sel, 1.0, 0.0)

    off = pl.multiple_of(qi * Q_BLOCK, Q_BLOCK)
    off_prev = pl.multiple_of(jnp.maximum(qi - 1, 0) * Q_BLOCK, Q_BLOCK)
    ckv = ckv_s[keys, :]
    lats = []
    for h in range(C_HEADS):
        qh = q_ref[0][:, h * C_LATENT:(h + 1) * C_LATENT].astype(BF16)
        lg_s[:, keys] = _dot_nt(qh, ckv) * scale + rb_ref[far_bucket, h]
        lg_s[:, pl.ds(off, Q_BLOCK)] = (_dot_nt(qh, ckv_s[pl.ds(off, Q_BLOCK), :]) * scale
                                        + bias_s[h][:, 0:Q_BLOCK])

        @pl.when(qi > 0)
        def _():
            lg_s[:, pl.ds(off_prev, Q_BLOCK)] = (_dot_nt(qh, ckv_s[pl.ds(off_prev, Q_BLOCK), :]) * scale
                                                 + bias_s[h][:, Q_BLOCK:2 * Q_BLOCK])

        logits = jnp.where(sel_s[:, keys] > 0.0, lg_s[:, keys], -jnp.inf)
        m = jnp.max(logits, axis=1, keepdims=True)
        p = jnp.exp(logits - m)
        p = p / jnp.sum(p, axis=1, keepdims=True)
        lats.append(_dot(p.astype(BF16), ckv).astype(BF16))
    o_ref[0] = _dot(jnp.concatenate(lats, axis=-1), wuv_ref[...])


def dsa_attention(proj, rel_bias, kv_norm, wuv_bd):
    b, s, _ = proj.shape
    topk = min(DSA_TOPK_MAX, s // 4)
    bk = jnp.asarray(_t5_bucket_table())
    col = lambda name, width: ODD_DST[name] // width
    return pl.pallas_call(
        functools.partial(_dsa_kernel, s_len=s, topk=topk),
        grid=(b, s // Q_BLOCK),
        in_specs=[pl.BlockSpec(memory_space=pltpu.SMEM),
                  pl.BlockSpec(bk.shape, lambda i, j: (0, 0)),
                  pl.BlockSpec((1, Q_BLOCK, 1024), lambda i, j: (i, j, col("q_lat", 1024))),
                  pl.BlockSpec((1, Q_BLOCK, 512), lambda i, j: (i, j, col("iq", 512))),
                  pl.BlockSpec((1, Q_BLOCK, LANES), lambda i, j: (i, j, col("iw", LANES))),
                  pl.BlockSpec((1, s, LANES), lambda i, j: (i, 0, col("c_kv", LANES))),
                  pl.BlockSpec((1, s, LANES), lambda i, j: (i, 0, col("ik", LANES))),
                  pl.BlockSpec((1, C_LATENT), lambda i, j: (0, 0)),
                  pl.BlockSpec(wuv_bd.shape, lambda i, j: (0, 0))],
        out_specs=pl.BlockSpec((1, Q_BLOCK, C_HEADS * C_HEAD_DIM), lambda i, j: (i, j, 0)),
        out_shape=jax.ShapeDtypeStruct((b, s, C_HEADS * C_HEAD_DIM), F32),
        scratch_shapes=[pltpu.VMEM((s, C_LATENT), BF16),
                        pltpu.VMEM((s, IDX_DIM), BF16),
                        pltpu.VMEM((C_HEADS, Q_BLOCK, 2 * Q_BLOCK), F32),
                        pltpu.VMEM((Q_BLOCK, s), I32),
                        pltpu.VMEM((Q_BLOCK, s), F32),
                        pltpu.VMEM((Q_BLOCK, s), F32)],
        compiler_params=_params("parallel", "arbitrary"),
        name="dsa_attention",
    )(rel_bias, bk, proj, proj, proj, proj, proj, kv_norm.reshape(1, C_LATENT), wuv_bd)


def _hgrn_kernel(gam_ref, nrm_ref, q_ref, f_ref, i_ref, g_ref, o_ref, st_ref, *, layer):
    c = pl.program_id(1)
    ch = HG_CHUNK

    @pl.when(c == 0)
    def _():
        st_ref[...] = jnp.zeros(st_ref.shape, F32)

    gam = gam_ref[...]
    pe = jnp.exp(gam - jnp.max(gam, axis=0, keepdims=True))
    pe = pe / jnp.sum(pe, axis=0, keepdims=True)
    lb = jnp.zeros((1, gam.shape[1]), F32)
    for i in range(layer):
        lb = lb + pe[i:i + 1]

    f = lb + (1.0 - lb) * jax.nn.sigmoid(f_ref[0])
    kk = 1.0 - f
    a = jnp.log(f)
    row = lax.broadcasted_iota(I32, (ch, 1), 0)
    sh = 1
    while sh < ch:
        a = a + jnp.where(row >= sh, pltpu.roll(a, sh, axis=0), 0.0)
        sh *= 2
    q = q_ref[0]
    inp = i_ref[0]
    a_end = a[ch - 1:ch, :]
    qe = q * jnp.exp(a)
    kd_end = kk * jnp.exp(a_end - a)
    trow = lax.broadcasted_iota(I32, (HG_SUB, 1), 0)

    outs = []
    for h in range(D_HEADS):
        sl = slice(h * D_KEY, (h + 1) * D_KEY)
        ah, qh, kh, ih = a[:, sl], q[:, sl], kk[:, sl], inp[:, sl]
        ih_bf = ih.astype(BF16)
        st_t = st_ref[h]
        o = _dot_nt(qe[:, sl].astype(BF16), st_t.astype(BF16))
        blocks = []
        for i in range(ch // HG_SUB):
            r0 = i * HG_SUB
            ai, qi, ki, ii = ah[r0:r0 + HG_SUB], qh[r0:r0 + HG_SUB], kh[r0:r0 + HG_SUB], ih[r0:r0 + HG_SUB]
            acc = jnp.zeros((HG_SUB, D_KEY), F32)
            if i > 0:
                r_i = ah[r0 - 1:r0]
                qd = qi * jnp.exp(ai - r_i)
                kd = kh[0:r0] * jnp.exp(r_i - ah[0:r0])
                s_off = _dot_nt(qd.astype(BF16), kd.astype(BF16))
                acc = _dot(s_off.astype(BF16), ih_bf[0:r0])
            for s in range(HG_SUB):
                e = jnp.exp(jnp.where(trow >= s, ai - ai[s:s + 1], -jnp.inf))
                sc = jnp.sum(qi * ki[s:s + 1] * e, axis=1, keepdims=True)
                acc = acc + sc * ii[s:s + 1]
            blocks.append(acc)
        o = o + jnp.concatenate(blocks, axis=0)
        st_ref[h] = st_t * jnp.exp(a_end[:, sl]) + _dot(ih.T.astype(BF16), kd_end[:, sl].astype(BF16))
        outs.append(_rms(o, nrm_ref[:, sl]))
    o_ref[0] = jnp.concatenate(outs, axis=-1) * jax.nn.silu(g_ref[0])


def hgrn2(proj, gamma, hg_norm, layer):
    b, s, _ = proj.shape
    w = D_HEADS * D_KEY
    col = lambda name: ODD_DST[name] // w
    spec = lambda name: pl.BlockSpec((1, HG_CHUNK, w), lambda i, j: (i, j, col(name)))
    return pl.pallas_call(
        functools.partial(_hgrn_kernel, layer=layer),
        grid=(b, s // HG_CHUNK),
        in_specs=[pl.BlockSpec(gamma.shape, lambda i, j: (0, 0)),
                  pl.BlockSpec((1, w), lambda i, j: (0, 0)),
                  spec("hq"), spec("hf"), spec("hi"), spec("hg")],
        out_specs=pl.BlockSpec((1, HG_CHUNK, w), lambda i, j: (i, j, 0)),
        out_shape=jax.ShapeDtypeStruct((b, s, w), F32),
        scratch_shapes=[pltpu.VMEM((D_HEADS, D_KEY, D_KEY), F32)],
        compiler_params=_params("parallel", "arbitrary"),
        name="hgrn2",
    )(gamma, hg_norm.reshape(1, w), proj, proj, proj, proj)


def _outproj_kernel(h_ref, a_ref, b_ref, w_ref, o_ref):
    ka = a_ref.shape[1]
    o_ref[...] = (h_ref[...] + _dot(a_ref[...].astype(BF16), w_ref[0:ka, :])
                  + _dot(b_ref[...].astype(BF16), w_ref[ka:, :]))


def outproj_residual(h, a, b, w, tm=512):
    m, d = h.shape
    return pl.pallas_call(
        _outproj_kernel,
        grid=(m // tm,),
        in_specs=[pl.BlockSpec((tm, d), lambda i: (i, 0)),
                  pl.BlockSpec((tm, a.shape[1]), lambda i: (i, 0)),
                  pl.BlockSpec((tm, b.shape[1]), lambda i: (i, 0)),
                  pl.BlockSpec(w.shape, lambda i: (0, 0))],
        out_specs=pl.BlockSpec((tm, d), lambda i: (i, 0)),
        out_shape=jax.ShapeDtypeStruct((m, d), F32),
        compiler_params=_params("parallel"),
        name="outproj_residual",
    )(h, a, b, w)


def _xattn_kernel(h_ref, g_ref, wq_ref, kv_ref, wo_ref, o_ref):
    x = h_ref[0]
    q = _dot(_rms(x, g_ref[...]).astype(BF16), wq_ref[...])
    kv = kv_ref[0]
    dm = X_HEADS * X_HEAD_DIM
    outs = []
    for h in range(X_HEADS):
        sl = slice(h * X_HEAD_DIM, (h + 1) * X_HEAD_DIM)
        logits = _dot_nt(q[:, sl].astype(BF16), kv[:, sl]) * (X_HEAD_DIM ** -0.5)
        p = jnp.exp(logits - jnp.max(logits, axis=1, keepdims=True))
        p = p / jnp.sum(p, axis=1, keepdims=True)
        outs.append(_dot(p.astype(BF16), kv[:, dm + h * X_HEAD_DIM:dm + (h + 1) * X_HEAD_DIM]).astype(BF16))
    o_ref[0] = x + _dot(jnp.concatenate(outs, axis=-1), wo_ref[...])


def memory_cross_attn(h, g, wq, kv, wo, ts=512):
    b, s, d = h.shape
    return pl.pallas_call(
        _xattn_kernel,
        grid=(b, s // ts),
        in_specs=[pl.BlockSpec((1, ts, d), lambda i, j: (i, j, 0)),
                  pl.BlockSpec((1, d), lambda i, j: (0, 0)),
                  pl.BlockSpec(wq.shape, lambda i, j: (0, 0)),
                  pl.BlockSpec((1,) + kv.shape[1:], lambda i, j: (i, 0, 0)),
                  pl.BlockSpec(wo.shape, lambda i, j: (0, 0))],
        out_specs=pl.BlockSpec((1, ts, d), lambda i, j: (i, j, 0)),
        out_shape=jax.ShapeDtypeStruct(h.shape, F32),
        compiler_params=_params("parallel", "parallel"),
        name="memory_cross_attn",
    )(h, g.reshape(1, d), wq, kv, wo)


def _extract_top(work, iota, rounds):
    rows = work.shape[0]
    rank = jnp.full(work.shape, NO_RANK, I32)
    vals = jnp.zeros((rounds, work.shape[1]), F32)
    riota = lax.broadcasted_iota(I32, (rounds, 1), 0)
    for r in range(rounds):
        m = jnp.max(work, axis=0, keepdims=True)
        first = jnp.min(jnp.where(work == m, iota, rows), axis=0, keepdims=True)
        hit = iota == first
        rank = jnp.where(hit, r, rank)
        work = jnp.where(hit, -jnp.inf, work)
        vals = jnp.where(riota == r, m, vals)
    return vals, rank


def _route_kernel(h_ref, g_ref, wq_ref, keys_ref, hn_ref, a_ref, n1_ref, b_ref, r2_ref, *, tt):
    hn = _rms(h_ref[...], g_ref[...])
    hn_ref[...] = hn.T.astype(BF16)
    q = _dot(hn.astype(BF16), wq_ref[...])
    k = PEER_TOPK
    iota_keys = lax.broadcasted_iota(I32, (N_KEYS, tt), 0)
    iota8 = lax.broadcasted_iota(I32, (8, 1), 0)
    n_full = 8
    cand_rows = k + 8 * (n_full - 1) + 8
    iota_cand = lax.broadcasted_iota(I32, (cand_rows, tt), 0)
    for h in range(PEER_HEADS):
        s1 = _dot_nt(keys_ref[h, 0], q[:, (2 * h) * N_KEYS:(2 * h + 1) * N_KEYS].astype(BF16))
        s2 = _dot_nt(keys_ref[h, 1], q[:, (2 * h + 1) * N_KEYS:(2 * h + 2) * N_KEYS].astype(BF16))
        a, rank1 = _extract_top(s1, iota_keys, k)
        b, rank2 = _extract_top(s2, iota_keys, k)
        blocks = [a[0:1] + b]
        for i in range(1, n_full):
            blocks.append(jnp.where(iota8 < k // (i + 1), a[i:i + 1] + b[0:8], -jnp.inf))
        blocks.append(a[8:16] + b[0:1])
        cand = jnp.concatenate(blocks, axis=0)
        _, crank = _extract_top(cand, iota_cand, k)
        chosen = crank < k
        z = jnp.sum(jnp.where(chosen, jnp.exp(cand - (a[0:1] + b[0:1])), 0.0), axis=0, keepdims=True)
        chosen_f = jnp.where(chosen, 1.0, 0.0)
        n_lo = jnp.zeros((8, tt), F32)
        n_lo = jnp.where(iota8 == 0, jnp.sum(chosen_f[0:k], axis=0, keepdims=True), n_lo)
        for i in range(1, n_full):
            r0 = k + 8 * (i - 1)
            n_lo = jnp.where(iota8 == i, jnp.sum(chosen_f[r0:r0 + 8], axis=0, keepdims=True), n_lo)
        n_hi = chosen_f[cand_rows - 8:cand_rows]
        n1 = jnp.zeros((N_KEYS, tt), F32)
        for r in range(k):
            n_r = n_lo[r:r + 1] if r < 8 else n_hi[r - 8:r - 7]
            n1 = jnp.where(rank1 == r, n_r, n1)
        a_ref[h] = jnp.exp(s1 - a[0:1]) / z
        n1_ref[h] = n1
        b_ref[h] = jnp.exp(s2 - b[0:1]).astype(BF16)
        r2_ref[h] = rank2.astype(F32).astype(BF16)


def peer_route(h2d, g, wq, keys, tt=128):
    t, d = h2d.shape
    tab = jax.ShapeDtypeStruct((PEER_HEADS, N_KEYS, t), F32)
    tab_bf = jax.ShapeDtypeStruct((PEER_HEADS, N_KEYS, t), BF16)
    tab_spec = pl.BlockSpec((PEER_HEADS, N_KEYS, tt), lambda i: (0, 0, i))
    return pl.pallas_call(
        functools.partial(_route_kernel, tt=tt),
        grid=(t // tt,),
        in_specs=[pl.BlockSpec((tt, d), lambda i: (i, 0)),
                  pl.BlockSpec((1, d), lambda i: (0, 0)),
                  pl.BlockSpec(wq.shape, lambda i: (0, 0)),
                  pl.BlockSpec(keys.shape, lambda i: (0, 0, 0, 0))],
        out_specs=[pl.BlockSpec((d, tt), lambda i: (0, i)), tab_spec, tab_spec, tab_spec, tab_spec],
        out_shape=[jax.ShapeDtypeStruct((d, t), BF16), tab, tab, tab_bf, tab_bf],
        compiler_params=_params("parallel"),
        name="peer_route",
    )(h2d, g.reshape(1, d), wq, keys)


def _peer_dense_kernel(h_ref, hn_ref, a_ref, n1_ref, b_ref, r2_ref, u_ref, vt_ref, fg_ref, o_ref, acc_ref, p0_ref, p1_ref,
                       *, te, final_norm):
    e = pl.program_id(1)
    n_tiles = pl.num_programs(1) - 1

    @pl.when(e == 0)
    def _():
        acc_ref[...] = jnp.zeros(acc_ref.shape, F32)
        p1_ref[...] = jnp.zeros(p1_ref.shape, BF16)

    hn = hn_ref[...]
    tt = hn.shape[1]
    keys_per_sub = PEER_SUB // N_KEYS
    tile = jnp.minimum(e, n_tiles - 1)

    def step(p_cur, p_prev):
        n_sub = te // PEER_SUB
        d_rows = acc_ref.shape[0] // n_sub

        def activations(sb):
            r0 = sb * PEER_SUB
            p_cur[r0:r0 + PEER_SUB, :] = jax.nn.gelu(_dot(u_ref[r0:r0 + PEER_SUB, :], hn)).astype(BF16)

        activations(0)
        for sb in range(n_sub):
            if sb + 1 < n_sub:
                activations(sb + 1)
            out_rows = slice(sb * d_rows, (sb + 1) * d_rows)
            acc_ref[out_rows, :] += _dot(vt_ref[out_rows, :], p_prev[...])
            r0 = sb * PEER_SUB
            for kb in range(keys_per_sub):
                i1 = tile * (te // N_KEYS) + sb * keys_per_sub + kb
                rows = slice(r0 + kb * N_KEYS, r0 + (kb + 1) * N_KEYS)
                a_full = [a_ref[h, pl.ds(i1, 1), :].astype(BF16) for h in range(PEER_HEADS)]
                n_full = [n1_ref[h, pl.ds(i1, 1), :].astype(BF16) for h in range(PEER_HEADS)]
                for c in range(tt // PEER_COLS):
                    cols = slice(c * PEER_COLS, (c + 1) * PEER_COLS)
                    gate = jnp.zeros((N_KEYS, PEER_COLS), BF16)
                    for h in range(PEER_HEADS):
                        gate = gate + jnp.where(r2_ref[h, :, cols] < n_full[h][:, cols],
                                                b_ref[h, :, cols] * a_full[h][:, cols], jnp.zeros((), BF16))
                    p_cur[rows, cols] = gate * p_cur[rows, cols]

    @pl.when(e % 2 == 0)
    def _():
        step(p0_ref, p1_ref)

    @pl.when(e % 2 == 1)
    def _():
        step(p1_ref, p0_ref)

    @pl.when(e == n_tiles)
    def _():
        out = h_ref[...] + acc_ref[...].T
        if final_norm:
            out = _rms(out, fg_ref[...])
        o_ref[...] = out


def peer_dense(h2d, hn_t, a, n1, b, r2, u, vt, final_g, final_norm, tt=512, te=1024):
    t, d = h2d.shape
    n_tiles = u.shape[0] // te
    tab_spec = pl.BlockSpec((PEER_HEADS, N_KEYS, tt), lambda i, j: (0, 0, i))
    return pl.pallas_call(
        functools.partial(_peer_dense_kernel, te=te, final_norm=final_norm),
        grid=(t // tt, n_tiles + 1),
        in_specs=[pl.BlockSpec((tt, d), lambda i, j: (i, 0)),
                  pl.BlockSpec((d, tt), lambda i, j: (0, i)),
                  tab_spec, tab_spec, tab_spec, tab_spec,
                  pl.BlockSpec((te, d), lambda i, j: (jnp.minimum(j, n_tiles - 1), 0)),
                  pl.BlockSpec((d, te), lambda i, j: (0, jnp.maximum(j - 1, 0))),
                  pl.BlockSpec((1, d), lambda i, j: (0, 0))],
        out_specs=pl.BlockSpec((tt, d), lambda i, j: (i, 0)),
        out_shape=jax.ShapeDtypeStruct((t, d), F32),
        scratch_shapes=[pltpu.VMEM((d, tt), F32), pltpu.VMEM((te, tt), BF16), pltpu.VMEM((te, tt), BF16)],
        compiler_params=_params("parallel", "arbitrary"),
        name="peer_dense",
    )(h2d, hn_t, a, n1, b, r2, u, vt, final_g.reshape(1, d))


def _pad_odd_w_in(w):
    out = jnp.zeros((w.shape[0], ODD_COLS), w.dtype)
    for name, (src, width) in ODD_SRC.items():
        out = lax.dynamic_update_slice(out, w[:, src:src + width], (0, ODD_DST[name]))
    return out


def _block_diag_uv(w_uv):
    out = jnp.zeros((C_HEADS * C_LATENT, C_HEADS * C_HEAD_DIM), w_uv.dtype)
    for h in range(C_HEADS):
        out = lax.dynamic_update_slice(out, w_uv[h], (h * C_LATENT, h * C_HEAD_DIM))
    return out


def kernel(x, mem, mix_norm, even_w_in, even_conv_w, even_pool_w, even_pool_scale, even_w_out, odd_w_in, odd_kv_norm, odd_w_uv, odd_hg_norm, odd_w_out, hgrn_gamma, rel_bias, mem_norm, xattn_norm, xattn_wq, xattn_wkv, xattn_wo, peer_norm, peer_wq, peer_keys, peer_u, peer_v, final_norm):
    bsz, s, d = x.shape
    t = bsz * s
    depth = mix_norm.shape[0]
    mem2d = mem.reshape(-1, d)
    h = x
    for l in range(depth):
        j = l // 2
        if l % 2 == 0:
            h = even_mixer(h, mix_norm[l], even_w_in[j].astype(BF16), even_conv_w[j].reshape(3, A_WIDTH),
                           even_pool_w[j].astype(BF16), even_pool_scale[j].reshape(1, -1),
                           even_w_out[j].astype(BF16))
        else:
            proj = norm_matmul(h.reshape(t, d), mix_norm[l], _pad_odd_w_in(odd_w_in[j]).astype(BF16), F32)
            proj = proj.reshape(bsz, s, ODD_COLS)
            y_c = dsa_attention(proj, rel_bias, odd_kv_norm[j], _block_diag_uv(odd_w_uv[j]).astype(BF16))
            y_d = hgrn2(proj, hgrn_gamma, odd_hg_norm[j], l)
            h = outproj_residual(h.reshape(t, d), y_c.reshape(t, -1), y_d.reshape(t, -1),
                                 odd_w_out[j].astype(BF16)).reshape(bsz, s, d)
        kv = norm_matmul(mem2d, mem_norm, xattn_wkv[l].astype(BF16), BF16).reshape(bsz, -1, 2 * d)
        h = memory_cross_attn(h, xattn_norm[l], xattn_wq[l].astype(BF16), kv, xattn_wo[l].astype(BF16))
        h2d = h.reshape(t, d)
        hn, a, n1, b, r2 = peer_route(h2d, peer_norm[l], peer_wq[l].astype(BF16), peer_keys[l].astype(BF16))
        h = peer_dense(h2d, hn, a, n1, b, r2, peer_u[l].astype(BF16), peer_v[l].T.astype(BF16),
                       final_norm, l == depth - 1).reshape(bsz, s, d)
    return h
```

```python
import functools
import math

import numpy as np
import jax
import jax.numpy as jnp
from jax import lax
from jax.experimental import pallas as pl
from jax.experimental.pallas import tpu as pltpu

F32 = jnp.float32
BF16 = jnp.bfloat16
I32 = jnp.int32

EPS = 1e-6
LANES = 128
INT_MIN = -(2 ** 31)
VMEM_LIMIT = 56 * 1024 * 1024

POOL_WINDOWS = (2, 4, 8, 16)
HALO = 16
A_WIDTH = 512
POOL_GROUP = 128

C_HEADS = 8
C_LATENT = 128
C_HEAD_DIM = 64
IDX_HEADS = 8
IDX_DIM = 64
DSA_TOPK_MAX = 256
Q_BLOCK = 128
DSA_TIERS = 4
REL_BUCKETS = 32
REL_MAX_DIST = 128

D_HEADS = 4
D_KEY = 128
HG_CHUNK = 128
HG_SUB = 16

X_HEADS = 4
X_HEAD_DIM = 256

N_KEYS = 128
PEER_HEADS = 8
PEER_TOPK = 16
NO_RANK = 127
PEER_SUB = 256
PEER_COLS = 256

ODD_COLS = 4096
ODD_SRC = dict(q_lat=(0, 1024), c_kv=(1024, 128), iq=(1152, 512), ik=(1664, 64), iw=(1728, 8),
               hq=(1736, 512), hf=(2248, 512), hi=(2760, 512), hg=(3272, 512))
ODD_DST = dict(q_lat=0, iq=1024, c_kv=1536, ik=1664, iw=1792, hq=2048, hf=2560, hi=3072, hg=3584)

NT_DIMS = (((1,), (1,)), ((), ()))


def _params(*sem):
    return pltpu.CompilerParams(dimension_semantics=sem, vmem_limit_bytes=VMEM_LIMIT)


def _rms(x, g):
    return x * lax.rsqrt(jnp.mean(x * x, axis=-1, keepdims=True) + EPS) * g


def _dot(a, b):
    return jnp.dot(a, b, preferred_element_type=F32)


def _dot_nt(a, b):
    return lax.dot_general(a, b, NT_DIMS, preferred_element_type=F32)


def _norm_matmul_kernel(x_ref, g_ref, w_ref, o_ref):
    y = _rms(x_ref[...], g_ref[...])
    o_ref[...] = _dot(y.astype(BF16), w_ref[...]).astype(o_ref.dtype)


def norm_matmul(x, g, w, out_dtype, tm=256):
    m, d = x.shape
    n = w.shape[1]
    return pl.pallas_call(
        _norm_matmul_kernel,
        grid=(m // tm,),
        in_specs=[pl.BlockSpec((tm, d), lambda i: (i, 0)),
                  pl.BlockSpec((1, d), lambda i: (0, 0)),
                  pl.BlockSpec((d, n), lambda i: (0, 0))],
        out_specs=pl.BlockSpec((tm, n), lambda i: (i, 0)),
        out_shape=jax.ShapeDtypeStruct((m, n), out_dtype),
        compiler_params=_params("parallel"),
        name="norm_matmul",
    )(x, g.reshape(1, d), w)


def _even_kernel(x_ref, g_ref, win_ref, cw_ref, pw_ref, ps_ref, wout_ref, o_ref, vbuf, pbuf, *, ts):
    j = pl.program_id(1)

    @pl.when(j == 0)
    def _():
        vbuf[0:HALO, :] = jnp.zeros((HALO, A_WIDTH), F32)
        pbuf[0:HALO, :] = jnp.zeros((HALO, A_WIDTH), F32)

    x = x_ref[0]
    proj = _dot(_rms(x, g_ref[...]).astype(BF16), win_ref[...])
    u = proj[:, 0:512]
    gate_c = proj[:, 512:1024]
    gate_b = proj[:, 1024:1536]
    pv = proj[:, 1536:2048]
    vbuf[HALO:HALO + ts, :] = gate_c * u
    pbuf[HALO:HALO + ts, :] = pv

    cw = cw_ref[...]
    z = (cw[2:3] * vbuf[HALO:HALO + ts, :] + cw[1:2] * vbuf[HALO - 1:HALO - 1 + ts, :]
         + cw[0:1] * vbuf[HALO - 2:HALO - 2 + ts, :])
    y_a = gate_b * z

    pos = j * ts + lax.broadcasted_iota(I32, (ts, 1), 0)
    y_b = []
    for gi, w in enumerate(POOL_WINDOWS):
        c0 = gi * POOL_GROUP
        acc = pbuf[HALO:HALO + ts, c0:c0 + POOL_GROUP]
        for k in range(1, w):
            acc = acc + pbuf[HALO - k:HALO - k + ts, c0:c0 + POOL_GROUP]
        cnt = jnp.minimum(pos + 1, w).astype(F32)
        pooled = acc / cnt - pv[:, c0:c0 + POOL_GROUP]
        y_b.append(_dot(pooled.astype(BF16), pw_ref[gi]))
    y_b = jnp.concatenate(y_b, axis=-1) * ps_ref[...]

    o_ref[0] = (x + _dot(y_a.astype(BF16), wout_ref[0:A_WIDTH, :])
                + _dot(y_b.astype(BF16), wout_ref[A_WIDTH:2 * A_WIDTH, :]))

    vbuf[0:HALO, :] = vbuf[ts:ts + HALO, :]
    pbuf[0:HALO, :] = pbuf[ts:ts + HALO, :]


def even_mixer(h, g, w_in, conv_w, pool_w, pool_scale, w_out, ts=512):
    b, s, d = h.shape
    return pl.pallas_call(
        functools.partial(_even_kernel, ts=ts),
        grid=(b, s // ts),
        in_specs=[pl.BlockSpec((1, ts, d), lambda i, j: (i, j, 0)),
                  pl.BlockSpec((1, d), lambda i, j: (0, 0)),
                  pl.BlockSpec(w_in.shape, lambda i, j: (0, 0)),
                  pl.BlockSpec(conv_w.shape, lambda i, j: (0, 0)),
                  pl.BlockSpec(pool_w.shape, lambda i, j: (0, 0, 0)),
                  pl.BlockSpec(pool_scale.shape, lambda i, j: (0, 0)),
                  pl.BlockSpec(w_out.shape, lambda i, j: (0, 0))],
        out_specs=pl.BlockSpec((1, ts, d), lambda i, j: (i, j, 0)),
        out_shape=jax.ShapeDtypeStruct(h.shape, F32),
        scratch_shapes=[pltpu.VMEM((HALO + ts, A_WIDTH), F32), pltpu.VMEM((HALO + ts, A_WIDTH), F32)],
        compiler_params=_params("parallel", "arbitrary"),
        name="even_mixer",
    )(h, g.reshape(1, d), w_in, conv_w, pool_w, pool_scale, w_out)


def _t5_bucket_table():
    t = np.arange(Q_BLOCK)[:, None]
    s = np.arange(Q_BLOCK)[None, :]
    dist = np.concatenate([t - s, t - s + Q_BLOCK], axis=1)
    max_exact = REL_BUCKETS // 2
    d = np.maximum(dist, 0)
    log_ratio = (np.log(np.maximum(d, 1).astype(np.float32) / np.float32(max_exact))
                 / np.float32(math.log(REL_MAX_DIST / max_exact)))
    large = max_exact + (log_ratio * np.float32(REL_BUCKETS - max_exact)).astype(np.int32)
    large = np.minimum(large, REL_BUCKETS - 1)
    return np.where(d < max_exact, d, large).astype(np.int32)


def _dsa_kernel(rb_ref, bk_ref, q_ref, iq_ref, iw_ref, ckv_ref, ik_ref, kvn_ref, wuv_ref, o_ref,
                ckv_s, ik_s, bias_s, key_s, sel_s, lg_s, *, s_len, topk):
    qi = pl.program_id(1)

    @pl.when(qi == 0)
    def _():
        ckv_s[...] = _rms(ckv_ref[0], kvn_ref[...]).astype(BF16)
        ik_s[...] = ik_ref[0][:, 0:IDX_DIM].astype(BF16)
        bk = bk_ref[...]
        for h in range(C_HEADS):
            tile = jnp.zeros(bk.shape, F32)
            for k in range(REL_BUCKETS):
                tile = jnp.where(bk == k, rb_ref[k, h], tile)
            bias_s[h] = tile

    n_q = s_len // Q_BLOCK
    tiers = DSA_TIERS if n_q % DSA_TIERS == 0 else 1
    for tier in range(tiers):
        lo, hi = tier * n_q // tiers, (tier + 1) * n_q // tiers

        @pl.when(jnp.logical_and(qi >= lo, qi < hi))
        def _(s_eff=hi * Q_BLOCK):
            _dsa_block(rb_ref, q_ref, iq_ref, iw_ref, wuv_ref, o_ref, ckv_s, ik_s, bias_s, key_s, sel_s, lg_s,
                       qi=qi, s_eff=s_eff, topk=topk)


def _dsa_block(rb_ref, q_ref, iq_ref, iw_ref, wuv_ref, o_ref, ckv_s, ik_s, bias_s, key_s, sel_s, lg_s, *, qi, s_eff, topk):
    scale = C_LATENT ** -0.5
    far_bucket = REL_BUCKETS - 1
    t_idx = qi * Q_BLOCK + lax.broadcasted_iota(I32, (Q_BLOCK, 1), 0)
    s_idx = lax.broadcasted_iota(I32, (1, s_eff), 1)
    causal = s_idx <= t_idx
    keys = slice(0, s_eff)

    iq = iq_ref[0]
    iw = iw_ref[0] * (IDX_HEADS ** -0.5)
    ik = ik_s[keys, :]
    score = jnp.zeros((Q_BLOCK, s_eff), F32)
    for h in range(IDX_HEADS):
        s_h = _dot_nt(iq[:, h * IDX_DIM:(h + 1) * IDX_DIM].astype(BF16), ik)
        score = score + jnp.maximum(s_h * (IDX_DIM ** -0.5), 0.0) * iw[:, h:h + 1]

    bits = lax.bitcast_convert_type(score + 0.0, I32)
    key = jnp.where(bits < 0, bits ^ jnp.int32(0x7FFFFFFF), bits)
    key_s[:, keys] = jnp.where(causal, key, jnp.int32(INT_MIN))

    def count(mask):
        return jnp.sum(jnp.where(mask, 1.0, 0.0), axis=1, keepdims=True)

    def thr_step(i, ans):
        cand = ans + lax.shift_left(jnp.int32(1), 31 - i)
        return jnp.where(count(key_s[:, keys] >= cand) >= topk, cand, ans)

    thr = lax.fori_loop(0, 32, thr_step, jnp.full((Q_BLOCK, 1), INT_MIN, I32))
    key = key_s[:, keys]
    gt = key > thr
    eq = key == thr
    need = topk - count(gt)
    tie_rows = jnp.logical_and(count(eq) > need, thr != jnp.int32(INT_MIN))
    any_tie = jnp.max(jnp.where(tie_rows, 1.0, 0.0)) > 0.0
    cut_bits = s_eff.bit_length()

    def tie_cut():
        def step(i, cut):
            cand = cut + lax.shift_left(jnp.int32(1), cut_bits - 1 - i)
            c = count(jnp.logical_and(key_s[:, keys] == thr, s_idx < cand))
            return jnp.where(c <= need, cand, cut)
        return lax.fori_loop(0, cut_bits, step, jnp.zeros((Q_BLOCK, 1), I32))

    cut = lax.cond(any_tie, tie_cut, lambda: jnp.full((Q_BLOCK, 1), s_eff, I32))
    sel = jnp.logical_and(causal, jnp.logical_or(gt, jnp.logical_and(eq, s_idx < cut)))
    sel_s[:, keys] = jnp.where(sel, 0.0, -jnp.inf)

    off = pl.multiple_of(qi * Q_BLOCK, Q_BLOCK)
    off_prev = pl.multiple_of(jnp.maximum(qi - 1, 0) * Q_BLOCK, Q_BLOCK)
    ckv = ckv_s[keys, :]
    lats = []
    for h in range(C_HEADS):
        qh = q_ref[0][:, h * C_LATENT:(h + 1) * C_LATENT].astype(BF16)
        far = rb_ref[far_bucket, h]
        lg_s[:, keys] = _dot_nt(qh, ckv) * scale + sel_s[:, keys]
        lg_s[:, pl.ds(off, Q_BLOCK)] = (_dot_nt(qh, ckv_s[pl.ds(off, Q_BLOCK), :]) * scale
                                        + (bias_s[h][:, 0:Q_BLOCK] - far) + sel_s[:, pl.ds(off, Q_BLOCK)])

        @pl.when(qi > 0)
        def _():
            lg_s[:, pl.ds(off_prev, Q_BLOCK)] = (_dot_nt(qh, ckv_s[pl.ds(off_prev, Q_BLOCK), :]) * scale
                                                 + (bias_s[h][:, Q_BLOCK:2 * Q_BLOCK] - far)
                                                 + sel_s[:, pl.ds(off_prev, Q_BLOCK)])

        logits = lg_s[:, keys]
        p = jnp.exp(logits - jnp.max(logits, axis=1, keepdims=True))
        lat = _dot(p.astype(BF16), ckv) / jnp.sum(p, axis=1, keepdims=True)
        lats.append(lat.astype(BF16))
    o_ref[0] = _dot(jnp.concatenate(lats, axis=-1), wuv_ref[...])


def dsa_attention(proj, rel_bias, kv_norm, wuv_bd):
    b, s, _ = proj.shape
    topk = min(DSA_TOPK_MAX, s // 4)
    bk = jnp.asarray(_t5_bucket_table())
    col = lambda name, width: ODD_DST[name] // width
    return pl.pallas_call(
        functools.partial(_dsa_kernel, s_len=s, topk=topk),
        grid=(b, s // Q_BLOCK),
        in_specs=[pl.BlockSpec(memory_space=pltpu.SMEM),
                  pl.BlockSpec(bk.shape, lambda i, j: (0, 0)),
                  pl.BlockSpec((1, Q_BLOCK, 1024), lambda i, j: (i, j, col("q_lat", 1024))),
                  pl.BlockSpec((1, Q_BLOCK, 512), lambda i, j: (i, j, col("iq", 512))),
                  pl.BlockSpec((1, Q_BLOCK, LANES), lambda i, j: (i, j, col("iw", LANES))),
                  pl.BlockSpec((1, s, LANES), lambda i, j: (i, 0, col("c_kv", LANES))),
                  pl.BlockSpec((1, s, LANES), lambda i, j: (i, 0, col("ik", LANES))),
                  pl.BlockSpec((1, C_LATENT), lambda i, j: (0, 0)),
                  pl.BlockSpec(wuv_bd.shape, lambda i, j: (0, 0))],
        out_specs=pl.BlockSpec((1, Q_BLOCK, C_HEADS * C_HEAD_DIM), lambda i, j: (i, j, 0)),
        out_shape=jax.ShapeDtypeStruct((b, s, C_HEADS * C_HEAD_DIM), F32),
        scratch_shapes=[pltpu.VMEM((s, C_LATENT), BF16),
                        pltpu.VMEM((s, IDX_DIM), BF16),
                        pltpu.VMEM((C_HEADS, Q_BLOCK, 2 * Q_BLOCK), F32),
                        pltpu.VMEM((Q_BLOCK, s), I32),
                        pltpu.VMEM((Q_BLOCK, s), F32),
                        pltpu.VMEM((Q_BLOCK, s), F32)],
        compiler_params=_params("parallel", "arbitrary"),
        name="dsa_attention",
    )(rel_bias, bk, proj, proj, proj, proj, proj, kv_norm.reshape(1, C_LATENT), wuv_bd)


def _hgrn_kernel(gam_ref, nrm_ref, q_ref, f_ref, i_ref, g_ref, o_ref, st_ref, *, layer):
    c = pl.program_id(1)
    ch = HG_CHUNK

    @pl.when(c == 0)
    def _():
        st_ref[...] = jnp.zeros(st_ref.shape, F32)

    gam = gam_ref[...]
    pe = jnp.exp(gam - jnp.max(gam, axis=0, keepdims=True))
    pe = pe / jnp.sum(pe, axis=0, keepdims=True)
    lb = jnp.zeros((1, gam.shape[1]), F32)
    for i in range(layer):
        lb = lb + pe[i:i + 1]

    f = lb + (1.0 - lb) * jax.nn.sigmoid(f_ref[0])
    kk = 1.0 - f
    a = jnp.log(f)
    row = lax.broadcasted_iota(I32, (ch, 1), 0)
    sh = 1
    while sh < ch:
        a = a + jnp.where(row >= sh, pltpu.roll(a, sh, axis=0), 0.0)
        sh *= 2
    q = q_ref[0]
    inp = i_ref[0]
    a_end = a[ch - 1:ch, :]
    qe = q * jnp.exp(a)
    kd_end = kk * jnp.exp(a_end - a)
    trow = lax.broadcasted_iota(I32, (HG_SUB, 1), 0)

    outs = []
    for h in range(D_HEADS):
        sl = slice(h * D_KEY, (h + 1) * D_KEY)
        ah, qh, kh, ih = a[:, sl], q[:, sl], kk[:, sl], inp[:, sl]
        ih_bf = ih.astype(BF16)
        st_t = st_ref[h]
        o = _dot_nt(qe[:, sl].astype(BF16), st_t.astype(BF16))
        blocks = []
        for i in range(ch // HG_SUB):
            r0 = i * HG_SUB
            ai, qi, ki, ii = ah[r0:r0 + HG_SUB], qh[r0:r0 + HG_SUB], kh[r0:r0 + HG_SUB], ih[r0:r0 + HG_SUB]
            acc = jnp.zeros((HG_SUB, D_KEY), F32)
            if i > 0:
                r_i = ah[r0 - 1:r0]
                qd = qi * jnp.exp(ai - r_i)
                kd = kh[0:r0] * jnp.exp(r_i - ah[0:r0])
                s_off = _dot_nt(qd.astype(BF16), kd.astype(BF16))
                acc = _dot(s_off.astype(BF16), ih_bf[0:r0])
            for s in range(HG_SUB):
                e = jnp.exp(jnp.where(trow >= s, ai - ai[s:s + 1], -jnp.inf))
                sc = jnp.sum(qi * ki[s:s + 1] * e, axis=1, keepdims=True)
                acc = acc + sc * ii[s:s + 1]
            blocks.append(acc)
        o = o + jnp.concatenate(blocks, axis=0)
        st_ref[h] = st_t * jnp.exp(a_end[:, sl]) + _dot(ih.T.astype(BF16), kd_end[:, sl].astype(BF16))
        outs.append(_rms(o, nrm_ref[:, sl]))
    o_ref[0] = jnp.concatenate(outs, axis=-1) * jax.nn.silu(g_ref[0])


def hgrn2(proj, gamma, hg_norm, layer):
    b, s, _ = proj.shape
    w = D_HEADS * D_KEY
    col = lambda name: ODD_DST[name] // w
    spec = lambda name: pl.BlockSpec((1, HG_CHUNK, w), lambda i, j: (i, j, col(name)))
    return pl.pallas_call(
        functools.partial(_hgrn_kernel, layer=layer),
        grid=(b, s // HG_CHUNK),
        in_specs=[pl.BlockSpec(gamma.shape, lambda i, j: (0, 0)),
                  pl.BlockSpec((1, w), lambda i, j: (0, 0)),
                  spec("hq"), spec("hf"), spec("hi"), spec("hg")],
        out_specs=pl.BlockSpec((1, HG_CHUNK, w), lambda i, j: (i, j, 0)),
        out_shape=jax.ShapeDtypeStruct((b, s, w), F32),
        scratch_shapes=[pltpu.VMEM((D_HEADS, D_KEY, D_KEY), F32)],
        compiler_params=_params("parallel", "arbitrary"),
        name="hgrn2",
    )(gamma, hg_norm.reshape(1, w), proj, proj, proj, proj)


def _outproj_kernel(h_ref, a_ref, b_ref, w_ref, o_ref):
    ka = a_ref.shape[1]
    o_ref[...] = (h_ref[...] + _dot(a_ref[...].astype(BF16), w_ref[0:ka, :])
                  + _dot(b_ref[...].astype(BF16), w_ref[ka:, :]))


def outproj_residual(h, a, b, w, tm=512):
    m, d = h.shape
    return pl.pallas_call(
        _outproj_kernel,
        grid=(m // tm,),
        in_specs=[pl.BlockSpec((tm, d), lambda i: (i, 0)),
                  pl.BlockSpec((tm, a.shape[1]), lambda i: (i, 0)),
                  pl.BlockSpec((tm, b.shape[1]), lambda i: (i, 0)),
                  pl.BlockSpec(w.shape, lambda i: (0, 0))],
        out_specs=pl.BlockSpec((tm, d), lambda i: (i, 0)),
        out_shape=jax.ShapeDtypeStruct((m, d), F32),
        compiler_params=_params("parallel"),
        name="outproj_residual",
    )(h, a, b, w)


def _xattn_kernel(h_ref, g_ref, wq_ref, kv_ref, wo_ref, o_ref):
    x = h_ref[0]
    q = _dot(_rms(x, g_ref[...]).astype(BF16), wq_ref[...])
    kv = kv_ref[0]
    dm = X_HEADS * X_HEAD_DIM
    outs = []
    for h in range(X_HEADS):
        sl = slice(h * X_HEAD_DIM, (h + 1) * X_HEAD_DIM)
        logits = _dot_nt(q[:, sl].astype(BF16), kv[:, sl]) * (X_HEAD_DIM ** -0.5)
        p = jnp.exp(logits - jnp.max(logits, axis=1, keepdims=True))
        p = p / jnp.sum(p, axis=1, keepdims=True)
        outs.append(_dot(p.astype(BF16), kv[:, dm + h * X_HEAD_DIM:dm + (h + 1) * X_HEAD_DIM]).astype(BF16))
    o_ref[0] = x + _dot(jnp.concatenate(outs, axis=-1), wo_ref[...])


def memory_cross_attn(h, g, wq, kv, wo, ts=512):
    b, s, d = h.shape
    return pl.pallas_call(
        _xattn_kernel,
        grid=(b, s // ts),
        in_specs=[pl.BlockSpec((1, ts, d), lambda i, j: (i, j, 0)),
                  pl.BlockSpec((1, d), lambda i, j: (0, 0)),
                  pl.BlockSpec(wq.shape, lambda i, j: (0, 0)),
                  pl.BlockSpec((1,) + kv.shape[1:], lambda i, j: (i, 0, 0)),
                  pl.BlockSpec(wo.shape, lambda i, j: (0, 0))],
        out_specs=pl.BlockSpec((1, ts, d), lambda i, j: (i, j, 0)),
        out_shape=jax.ShapeDtypeStruct(h.shape, F32),
        compiler_params=_params("parallel", "parallel"),
        name="memory_cross_attn",
    )(h, g.reshape(1, d), wq, kv, wo)


def _extract_top(work, iota, rounds):
    rows = work.shape[0]
    rank = jnp.full(work.shape, NO_RANK, I32)
    vals = jnp.zeros((rounds, work.shape[1]), F32)
    riota = lax.broadcasted_iota(I32, (rounds, 1), 0)
    for r in range(rounds):
        m = jnp.max(work, axis=0, keepdims=True)
        first = jnp.min(jnp.where(work == m, iota, rows), axis=0, keepdims=True)
        hit = iota == first
        rank = jnp.where(hit, r, rank)
        work = jnp.where(hit, -jnp.inf, work)
        vals = jnp.where(riota == r, m, vals)
    return vals, rank


def _batcher_pairs(n):
    pairs = []
    p = 1
    while p < n:
        k = p
        while k >= 1:
            for j in range(k % p, n - k, 2 * k):
                for i in range(min(k, n - j - k)):
                    if (i + j) // (2 * p) == (i + j + k) // (2 * p):
                        pairs.append((i + j, i + j + k))
            k //= 2
        p *= 2
    return pairs


_SORT16 = _batcher_pairs(PEER_TOPK)


def _compare_exchange(w, i, j):
    a, b = w[i], w[j]
    if b is None:
        return
    if a is None:
        w[i], w[j] = b, None
        return
    w[i], w[j] = jnp.maximum(a, b), jnp.minimum(a, b)


def _sorted_top(slabs):
    k = PEER_TOPK
    w = list(slabs) + [None] * (k - len(slabs))
    for i, j in _SORT16:
        _compare_exchange(w, i, j)
    for shift in (4, 2, 1):
        r = [None if x is None else pltpu.roll(x, shift, axis=0) for x in w]
        t = []
        for i in range(k):
            a, b = w[i], r[k - 1 - i]
            t.append(b if a is None else a if b is None else jnp.maximum(a, b))
        w = t
        d = k // 2
        while d >= 1:
            for i in range(k):
                if i & d == 0:
                    _compare_exchange(w, i, i + d)
            d //= 2
    return w


def _slabs(x):
    return [x[8 * j:8 * j + 8] for j in range(x.shape[0] // 8)]


def _count_ge(slabs, thr):
    acc = None
    for x in slabs:
        c = jnp.where(x >= thr, 1.0, 0.0)
        acc = c if acc is None else acc + c
    return jnp.sum(acc, axis=0, keepdims=True)


def _route_head_fast(s1, s2, iota8):
    k = PEER_TOPK
    c1, c2 = _slabs(s1), _slabs(s2)
    a, b = _sorted_top(c1), _sorted_top(c2)
    tie = jnp.logical_or(_count_ge(c1, a[k - 1]) != k, _count_ge(c2, b[k - 1]) != k)
    for i in range(k - 1):
        tie = jnp.logical_or(tie, jnp.logical_or(a[i][0:1] <= a[i + 1][0:1], b[i][0:1] <= b[i + 1][0:1]))

    def stack8(vals):
        out = vals[0]
        for r in range(1, 8):
            out = jnp.where(iota8 == r, vals[r], out)
        return out

    a_hi, b_lo, b_hi = stack8(a[8:]), stack8(b[:8]), stack8(b[8:])
    cand = [a[0] + b_lo, a[0] + b_hi]
    for i in range(1, 8):
        cand.append(jnp.where(iota8 < k // (i + 1), a[i] + b_lo, -jnp.inf))
    cand.append(a_hi + b[0])
    thr = _sorted_top(cand)[k - 1]
    tie = jnp.logical_or(tie, _count_ge(cand, thr) != k)
    chosen = [jnp.where(x >= thr, 1.0, 0.0) for x in cand]
    top = a[0] + b[0]
    z = None
    for x, m in zip(cand, chosen):
        e = m * jnp.exp(x - top)
        z = e if z is None else z + e
    z = jnp.sum(z, axis=0, keepdims=True)
    n = [jnp.sum(chosen[0] + chosen[1], axis=0, keepdims=True)]
    n += [jnp.sum(chosen[i + 1], axis=0, keepdims=True) for i in range(1, 8)]
    n += [chosen[9][r:r + 1] for r in range(8)]
    n = [jnp.broadcast_to(x, (8, x.shape[1])) for x in n]
    n1, r2 = [], []
    for x1, x2 in zip(c1, c2):
        acc_n = jnp.zeros(x1.shape, F32)
        acc_r = jnp.full(x2.shape, float(NO_RANK), F32)
        for r in range(k - 1, -1, -1):
            acc_n = jnp.where(x1 >= a[r], n[r], acc_n)
            acc_r = jnp.where(x2 >= b[r], float(r), acc_r)
        n1.append(acc_n)
        r2.append(acc_r)
    a_tab = jnp.exp(s1 - a[0][0:1]) / z
    b_tab = jnp.exp(s2 - b[0][0:1])
    return a_tab, jnp.concatenate(n1, axis=0), b_tab, jnp.concatenate(r2, axis=0), tie


def _route_head_exact(s1, s2, iota_keys, iota8, iota_cand):
    k = PEER_TOPK
    tt = s1.shape[1]
    n_full = 8
    cand_rows = k + 8 * (n_full - 1) + 8
    a, rank1 = _extract_top(s1, iota_keys, k)
    b, rank2 = _extract_top(s2, iota_keys, k)
    blocks = [a[0:1] + b]
    for i in range(1, n_full):
        blocks.append(jnp.where(iota8 < k // (i + 1), a[i:i + 1] + b[0:8], -jnp.inf))
    blocks.append(a[8:16] + b[0:1])
    cand = jnp.concatenate(blocks, axis=0)
    _, crank = _extract_top(cand, iota_cand, k)
    chosen = crank < k
    z = jnp.sum(jnp.where(chosen, jnp.exp(cand - (a[0:1] + b[0:1])), 0.0), axis=0, keepdims=True)
    chosen_f = jnp.where(chosen, 1.0, 0.0)
    n_lo = jnp.zeros((8, tt), F32)
    n_lo = jnp.where(iota8 == 0, jnp.sum(chosen_f[0:k], axis=0, keepdims=True), n_lo)
    for i in range(1, n_full):
        r0 = k + 8 * (i - 1)
        n_lo = jnp.where(iota8 == i, jnp.sum(chosen_f[r0:r0 + 8], axis=0, keepdims=True), n_lo)
    n_hi = chosen_f[cand_rows - 8:cand_rows]
    n1 = jnp.zeros((N_KEYS, tt), F32)
    for r in range(k):
        n_r = n_lo[r:r + 1] if r < 8 else n_hi[r - 8:r - 7]
        n1 = jnp.where(rank1 == r, n_r, n1)
    return jnp.exp(s1 - a[0:1]) / z, n1, jnp.exp(s2 - b[0:1]), rank2.astype(F32)


def _route_kernel(h_ref, g_ref, wq_ref, keys_ref, hn_ref, a_ref, n1_ref, b_ref, r2_ref, q_s, *, tt):
    hn = _rms(h_ref[...], g_ref[...])
    hn_ref[...] = hn.T.astype(BF16)
    q_s[...] = _dot(hn.astype(BF16), wq_ref[...]).astype(BF16)
    iota8 = lax.broadcasted_iota(I32, (8, 1), 0)

    def scores(h):
        s1 = _dot_nt(keys_ref[h, 0], q_s[:, (2 * h) * N_KEYS:(2 * h + 1) * N_KEYS])
        s2 = _dot_nt(keys_ref[h, 1], q_s[:, (2 * h + 1) * N_KEYS:(2 * h + 2) * N_KEYS])
        return s1, s2

    def store(h, tabs):
        a_ref[h], n1_ref[h] = tabs[0], tabs[1]
        b_ref[h], r2_ref[h] = tabs[2].astype(BF16), tabs[3].astype(BF16)

    tie = jnp.zeros((1, tt), jnp.bool_)
    for h in range(PEER_HEADS):
        tabs = _route_head_fast(*scores(h), iota8)
        store(h, tabs[:4])
        tie = jnp.logical_or(tie, tabs[4])

    @pl.when(jnp.max(jnp.where(tie, 1.0, 0.0)) > 0.0)
    def _():
        iota_keys = lax.broadcasted_iota(I32, (N_KEYS, tt), 0)
        iota_cand = lax.broadcasted_iota(I32, (PEER_TOPK + 8 * 8, tt), 0)
        for h in range(PEER_HEADS):
            store(h, _route_head_exact(*scores(h), iota_keys, iota8, iota_cand))


def peer_route(h2d, g, wq, keys, tt=128):
    t, d = h2d.shape
    tab = jax.ShapeDtypeStruct((PEER_HEADS, N_KEYS, t), F32)
    tab_bf = jax.ShapeDtypeStruct((PEER_HEADS, N_KEYS, t), BF16)
    tab_spec = pl.BlockSpec((PEER_HEADS, N_KEYS, tt), lambda i: (0, 0, i))
    return pl.pallas_call(
        functools.partial(_route_kernel, tt=tt),
        grid=(t // tt,),
        in_specs=[pl.BlockSpec((tt, d), lambda i: (i, 0)),
                  pl.BlockSpec((1, d), lambda i: (0, 0)),
                  pl.BlockSpec(wq.shape, lambda i: (0, 0)),
                  pl.BlockSpec(keys.shape, lambda i: (0, 0, 0, 0))],
        out_specs=[pl.BlockSpec((d, tt), lambda i: (0, i)), tab_spec, tab_spec, tab_spec, tab_spec],
        out_shape=[jax.ShapeDtypeStruct((d, t), BF16), tab, tab, tab_bf, tab_bf],
        scratch_shapes=[pltpu.VMEM((tt, wq.shape[1]), BF16)],
        compiler_params=_params("parallel"),
        name="peer_route",
    )(h2d, g.reshape(1, d), wq, keys)


def _peer_dense_kernel(h_ref, hn_ref, a_ref, n1_ref, b_ref, r2_ref, u_ref, vt_ref, fg_ref, o_ref, acc_ref, p0_ref, p1_ref,
                       *, te, final_norm):
    e = pl.program_id(1)
    n_tiles = pl.num_programs(1) - 1

    @pl.when(e == 0)
    def _():
        acc_ref[...] = jnp.zeros(acc_ref.shape, F32)
        p1_ref[...] = jnp.zeros(p1_ref.shape, BF16)

    hn = hn_ref[...]
    tt = hn.shape[1]
    keys_per_sub = PEER_SUB // N_KEYS
    tile = jnp.minimum(e, n_tiles - 1)

    def step(p_cur, p_prev):
        n_sub = te // PEER_SUB
        d_rows = acc_ref.shape[0] // n_sub

        def activations(sb):
            r0 = sb * PEER_SUB
            p_cur[r0:r0 + PEER_SUB, :] = jax.nn.gelu(_dot(u_ref[r0:r0 + PEER_SUB, :], hn)).astype(BF16)

        activations(0)
        for sb in range(n_sub):
            if sb + 1 < n_sub:
                activations(sb + 1)
            out_rows = slice(sb * d_rows, (sb + 1) * d_rows)
            acc_ref[out_rows, :] += _dot(vt_ref[out_rows, :], p_prev[...])
            r0 = sb * PEER_SUB
            for kb in range(keys_per_sub):
                i1 = tile * (te // N_KEYS) + sb * keys_per_sub + kb
                rows = slice(r0 + kb * N_KEYS, r0 + (kb + 1) * N_KEYS)
                a_full = [a_ref[h, pl.ds(i1, 1), :].astype(BF16) for h in range(PEER_HEADS)]
                n_full = [n1_ref[h, pl.ds(i1, 1), :].astype(BF16) for h in range(PEER_HEADS)]
                for c in range(tt // PEER_COLS):
                    cols = slice(c * PEER_COLS, (c + 1) * PEER_COLS)
                    gate = jnp.zeros((N_KEYS, PEER_COLS), BF16)
                    for h in range(PEER_HEADS):
                        gate = gate + jnp.where(r2_ref[h, :, cols] < n_full[h][:, cols],
                                                b_ref[h, :, cols] * a_full[h][:, cols], jnp.zeros((), BF16))
                    p_cur[rows, cols] = gate * p_cur[rows, cols]

    @pl.when(e % 2 == 0)
    def _():
        step(p0_ref, p1_ref)

    @pl.when(e % 2 == 1)
    def _():
        step(p1_ref, p0_ref)

    @pl.when(e == n_tiles)
    def _():
        out = h_ref[...] + acc_ref[...].T
        if final_norm:
            out = _rms(out, fg_ref[...])
        o_ref[...] = out


def peer_dense(h2d, hn_t, a, n1, b, r2, u, vt, final_g, final_norm, tt=512, te=1024):
    t, d = h2d.shape
    n_tiles = u.shape[0] // te
    tab_spec = pl.BlockSpec((PEER_HEADS, N_KEYS, tt), lambda i, j: (0, 0, i))
    return pl.pallas_call(
        functools.partial(_peer_dense_kernel, te=te, final_norm=final_norm),
        grid=(t // tt, n_tiles + 1),
        in_specs=[pl.BlockSpec((tt, d), lambda i, j: (i, 0)),
                  pl.BlockSpec((d, tt), lambda i, j: (0, i)),
                  tab_spec, tab_spec, tab_spec, tab_spec,
                  pl.BlockSpec((te, d), lambda i, j: (jnp.minimum(j, n_tiles - 1), 0)),
                  pl.BlockSpec((d, te), lambda i, j: (0, jnp.maximum(j - 1, 0))),
                  pl.BlockSpec((1, d), lambda i, j: (0, 0))],
        out_specs=pl.BlockSpec((tt, d), lambda i, j: (i, 0)),
        out_shape=jax.ShapeDtypeStruct((t, d), F32),
        scratch_shapes=[pltpu.VMEM((d, tt), F32), pltpu.VMEM((te, tt), BF16), pltpu.VMEM((te, tt), BF16)],
        compiler_params=_params("parallel", "arbitrary"),
        name="peer_dense",
    )(h2d, hn_t, a, n1, b, r2, u, vt, final_g.reshape(1, d))


def _pad_odd_w_in(w):
    out = jnp.zeros((w.shape[0], ODD_COLS), w.dtype)
    for name, (src, width) in ODD_SRC.items():
        out = lax.dynamic_update_slice(out, w[:, src:src + width], (0, ODD_DST[name]))
    return out


def _block_diag_uv(w_uv):
    out = jnp.zeros((C_HEADS * C_LATENT, C_HEADS * C_HEAD_DIM), w_uv.dtype)
    for h in range(C_HEADS):
        out = lax.dynamic_update_slice(out, w_uv[h], (h * C_LATENT, h * C_HEAD_DIM))
    return out


def kernel(x, mem, mix_norm, even_w_in, even_conv_w, even_pool_w, even_pool_scale, even_w_out, odd_w_in, odd_kv_norm, odd_w_uv, odd_hg_norm, odd_w_out, hgrn_gamma, rel_bias, mem_norm, xattn_norm, xattn_wq, xattn_wkv, xattn_wo, peer_norm, peer_wq, peer_keys, peer_u, peer_v, final_norm):
    bsz, s, d = x.shape
    t = bsz * s
    depth = mix_norm.shape[0]
    mem2d = mem.reshape(-1, d)
    h = x
    for l in range(depth):
        j = l // 2
        if l % 2 == 0:
            h = even_mixer(h, mix_norm[l], even_w_in[j].astype(BF16), even_conv_w[j].reshape(3, A_WIDTH),
                           even_pool_w[j].astype(BF16), even_pool_scale[j].reshape(1, -1),
                           even_w_out[j].astype(BF16))
        else:
            proj = norm_matmul(h.reshape(t, d), mix_norm[l], _pad_odd_w_in(odd_w_in[j]).astype(BF16), F32)
            proj = proj.reshape(bsz, s, ODD_COLS)
            y_c = dsa_attention(proj, rel_bias, odd_kv_norm[j], _block_diag_uv(odd_w_uv[j]).astype(BF16))
            y_d = hgrn2(proj, hgrn_gamma, odd_hg_norm[j], l)
            h = outproj_residual(h.reshape(t, d), y_c.reshape(t, -1), y_d.reshape(t, -1),
                                 odd_w_out[j].astype(BF16)).reshape(bsz, s, d)
        kv = norm_matmul(mem2d, mem_norm, xattn_wkv[l].astype(BF16), BF16).reshape(bsz, -1, 2 * d)
        h = memory_cross_attn(h, xattn_norm[l], xattn_wq[l].astype(BF16), kv, xattn_wo[l].astype(BF16))
        h2d = h.reshape(t, d)
        hn, a, n1, b, r2 = peer_route(h2d, peer_norm[l], peer_wq[l].astype(BF16), peer_keys[l].astype(BF16))
        h = peer_dense(h2d, hn, a, n1, b, r2, peer_u[l].astype(BF16), peer_v[l].T.astype(BF16),
                       final_norm, l == depth - 1).reshape(bsz, s, d)
    return h
```

```python
import functools
import math

import numpy as np
import jax
import jax.numpy as jnp
from jax import lax
from jax.experimental import pallas as pl
from jax.experimental.pallas import tpu as pltpu

F32 = jnp.float32
BF16 = jnp.bfloat16
I32 = jnp.int32

EPS = 1e-6
LANES = 128
INT_MIN = -(2 ** 31)
VMEM_LIMIT = 56 * 1024 * 1024

POOL_WINDOWS = (2, 4, 8, 16)
HALO = 16
A_WIDTH = 512
POOL_GROUP = 128

C_HEADS = 8
C_LATENT = 128
C_HEAD_DIM = 64
IDX_HEADS = 8
IDX_DIM = 64
DSA_TOPK_MAX = 256
Q_BLOCK = 128
DSA_TIERS = 8
REL_BUCKETS = 32
REL_MAX_DIST = 128

D_HEADS = 4
D_KEY = 128
HG_CHUNK = 128
HG_SUB = 16

X_HEADS = 4
X_HEAD_DIM = 256

N_KEYS = 128
PEER_HEADS = 8
PEER_TOPK = 16
NO_RANK = 127
PEER_SUB = 256
PEER_COLS = 256

ODD_COLS = 4096
ODD_SRC = dict(q_lat=(0, 1024), c_kv=(1024, 128), iq=(1152, 512), ik=(1664, 64), iw=(1728, 8),
               hq=(1736, 512), hf=(2248, 512), hi=(2760, 512), hg=(3272, 512))
ODD_DST = dict(q_lat=0, iq=1024, c_kv=1536, ik=1664, iw=1792, hq=2048, hf=2560, hi=3072, hg=3584)

NT_DIMS = (((1,), (1,)), ((), ()))


def _params(*sem):
    return pltpu.CompilerParams(dimension_semantics=sem, vmem_limit_bytes=VMEM_LIMIT)


def _rms(x, g):
    return x * lax.rsqrt(jnp.mean(x * x, axis=-1, keepdims=True) + EPS) * g


def _dot(a, b):
    return jnp.dot(a, b, preferred_element_type=F32)


def _dot_nt(a, b):
    return lax.dot_general(a, b, NT_DIMS, preferred_element_type=F32)


def _norm_matmul_kernel(x_ref, g_ref, w_ref, o_ref):
    y = _rms(x_ref[...], g_ref[...])
    o_ref[...] = _dot(y.astype(BF16), w_ref[...]).astype(o_ref.dtype)


def norm_matmul(x, g, w, out_dtype, tm=256):
    m, d = x.shape
    n = w.shape[1]
    return pl.pallas_call(
        _norm_matmul_kernel,
        grid=(m // tm,),
        in_specs=[pl.BlockSpec((tm, d), lambda i: (i, 0)),
                  pl.BlockSpec((1, d), lambda i: (0, 0)),
                  pl.BlockSpec((d, n), lambda i: (0, 0))],
        out_specs=pl.BlockSpec((tm, n), lambda i: (i, 0)),
        out_shape=jax.ShapeDtypeStruct((m, n), out_dtype),
        compiler_params=_params("parallel"),
        name="norm_matmul",
    )(x, g.reshape(1, d), w)


def _even_kernel(x_ref, g_ref, win_ref, cw_ref, pw_ref, ps_ref, wout_ref, o_ref, vbuf, pbuf, *, ts):
    j = pl.program_id(1)

    @pl.when(j == 0)
    def _():
        vbuf[0:HALO, :] = jnp.zeros((HALO, A_WIDTH), F32)
        pbuf[0:HALO, :] = jnp.zeros((HALO, A_WIDTH), F32)

    x = x_ref[0]
    proj = _dot(_rms(x, g_ref[...]).astype(BF16), win_ref[...])
    u = proj[:, 0:512]
    gate_c = proj[:, 512:1024]
    gate_b = proj[:, 1024:1536]
    pv = proj[:, 1536:2048]
    vbuf[HALO:HALO + ts, :] = gate_c * u
    pbuf[HALO:HALO + ts, :] = pv

    cw = cw_ref[...]
    z = (cw[2:3] * vbuf[HALO:HALO + ts, :] + cw[1:2] * vbuf[HALO - 1:HALO - 1 + ts, :]
         + cw[0:1] * vbuf[HALO - 2:HALO - 2 + ts, :])
    y_a = gate_b * z

    pos = j * ts + lax.broadcasted_iota(I32, (ts, 1), 0)
    y_b = []
    for gi, w in enumerate(POOL_WINDOWS):
        c0 = gi * POOL_GROUP
        acc = pbuf[HALO:HALO + ts, c0:c0 + POOL_GROUP]
        for k in range(1, w):
            acc = acc + pbuf[HALO - k:HALO - k + ts, c0:c0 + POOL_GROUP]
        cnt = jnp.minimum(pos + 1, w).astype(F32)
        pooled = acc / cnt - pv[:, c0:c0 + POOL_GROUP]
        y_b.append(_dot(pooled.astype(BF16), pw_ref[gi]))
    y_b = jnp.concatenate(y_b, axis=-1) * ps_ref[...]

    o_ref[0] = (x + _dot(y_a.astype(BF16), wout_ref[0:A_WIDTH, :])
                + _dot(y_b.astype(BF16), wout_ref[A_WIDTH:2 * A_WIDTH, :]))

    vbuf[0:HALO, :] = vbuf[ts:ts + HALO, :]
    pbuf[0:HALO, :] = pbuf[ts:ts + HALO, :]


def even_mixer(h, g, w_in, conv_w, pool_w, pool_scale, w_out, ts=512):
    b, s, d = h.shape
    return pl.pallas_call(
        functools.partial(_even_kernel, ts=ts),
        grid=(b, s // ts),
        in_specs=[pl.BlockSpec((1, ts, d), lambda i, j: (i, j, 0)),
                  pl.BlockSpec((1, d), lambda i, j: (0, 0)),
                  pl.BlockSpec(w_in.shape, lambda i, j: (0, 0)),
                  pl.BlockSpec(conv_w.shape, lambda i, j: (0, 0)),
                  pl.BlockSpec(pool_w.shape, lambda i, j: (0, 0, 0)),
                  pl.BlockSpec(pool_scale.shape, lambda i, j: (0, 0)),
                  pl.BlockSpec(w_out.shape, lambda i, j: (0, 0))],
        out_specs=pl.BlockSpec((1, ts, d), lambda i, j: (i, j, 0)),
        out_shape=jax.ShapeDtypeStruct(h.shape, F32),
        scratch_shapes=[pltpu.VMEM((HALO + ts, A_WIDTH), F32), pltpu.VMEM((HALO + ts, A_WIDTH), F32)],
        compiler_params=_params("parallel", "arbitrary"),
        name="even_mixer",
    )(h, g.reshape(1, d), w_in, conv_w, pool_w, pool_scale, w_out)


def _t5_bucket_table():
    t = np.arange(Q_BLOCK)[:, None]
    s = np.arange(Q_BLOCK)[None, :]
    dist = np.concatenate([t - s, t - s + Q_BLOCK], axis=1)
    max_exact = REL_BUCKETS // 2
    d = np.maximum(dist, 0)
    log_ratio = (np.log(np.maximum(d, 1).astype(np.float32) / np.float32(max_exact))
                 / np.float32(math.log(REL_MAX_DIST / max_exact)))
    large = max_exact + (log_ratio * np.float32(REL_BUCKETS - max_exact)).astype(np.int32)
    large = np.minimum(large, REL_BUCKETS - 1)
    return np.where(d < max_exact, d, large).astype(np.int32)


def _dsa_kernel(rb_ref, bk_ref, q_ref, iq_ref, iw_ref, ckv_ref, ik_ref, kvn_ref, wuv_ref, o_ref,
                ckv_s, ik_s, bias_s, key_s, sel_s, lg_s, *, s_len, topk):
    qi = pl.program_id(1)

    @pl.when(qi == 0)
    def _():
        ckv_s[...] = _rms(ckv_ref[0], kvn_ref[...]).astype(BF16)
        ik_s[...] = ik_ref[0][:, 0:IDX_DIM].astype(BF16)
        bk = bk_ref[...]
        for h in range(C_HEADS):
            tile = jnp.zeros(bk.shape, F32)
            for k in range(REL_BUCKETS):
                tile = jnp.where(bk == k, rb_ref[k, h], tile)
            bias_s[h] = tile

    n_q = s_len // Q_BLOCK
    tiers = DSA_TIERS if n_q % DSA_TIERS == 0 else 1
    for tier in range(tiers):
        lo, hi = tier * n_q // tiers, (tier + 1) * n_q // tiers

        @pl.when(jnp.logical_and(qi >= lo, qi < hi))
        def _(s_eff=hi * Q_BLOCK):
            _dsa_block(rb_ref, q_ref, iq_ref, iw_ref, wuv_ref, o_ref, ckv_s, ik_s, bias_s, key_s, sel_s, lg_s,
                       qi=qi, s_eff=s_eff, topk=topk)


def _dsa_block(rb_ref, q_ref, iq_ref, iw_ref, wuv_ref, o_ref, ckv_s, ik_s, bias_s, key_s, sel_s, lg_s, *, qi, s_eff, topk):
    scale = C_LATENT ** -0.5
    far_bucket = REL_BUCKETS - 1
    t_idx = qi * Q_BLOCK + lax.broadcasted_iota(I32, (Q_BLOCK, 1), 0)
    s_idx = lax.broadcasted_iota(I32, (1, s_eff), 1)
    causal = s_idx <= t_idx
    keys = slice(0, s_eff)

    iq = iq_ref[0]
    iw = iw_ref[0] * (IDX_HEADS ** -0.5 * IDX_DIM ** -0.5)
    ik = ik_s[keys, :]
    score = jnp.zeros((Q_BLOCK, s_eff), F32)
    for h in range(IDX_HEADS):
        s_h = _dot_nt(iq[:, h * IDX_DIM:(h + 1) * IDX_DIM].astype(BF16), ik)
        score = score + jnp.maximum(s_h, 0.0) * iw[:, h:h + 1]

    bits = lax.bitcast_convert_type(score + 0.0, I32)
    key = jnp.where(bits < 0, bits ^ jnp.int32(0x7FFFFFFF), bits)
    key_s[:, keys] = jnp.where(causal, key, jnp.int32(INT_MIN))

    def count(mask):
        return jnp.sum(jnp.where(mask, 1.0, 0.0), axis=1, keepdims=True)

    def thr_step(i, ans):
        cand = ans + lax.shift_left(jnp.int32(1), 31 - i)
        return jnp.where(count(key_s[:, keys] >= cand) >= topk, cand, ans)

    thr = lax.fori_loop(0, 32, thr_step, jnp.full((Q_BLOCK, 1), INT_MIN, I32))
    key = key_s[:, keys]
    gt = key > thr
    eq = key == thr
    need = topk - count(gt)
    tie_rows = jnp.logical_and(count(eq) > need, thr != jnp.int32(INT_MIN))
    any_tie = jnp.max(jnp.where(tie_rows, 1.0, 0.0)) > 0.0
    cut_bits = s_eff.bit_length()

    def tie_cut():
        def step(i, cut):
            cand = cut + lax.shift_left(jnp.int32(1), cut_bits - 1 - i)
            c = count(jnp.logical_and(key_s[:, keys] == thr, s_idx < cand))
            return jnp.where(c <= need, cand, cut)
        return lax.fori_loop(0, cut_bits, step, jnp.zeros((Q_BLOCK, 1), I32))

    cut = lax.cond(any_tie, tie_cut, lambda: jnp.full((Q_BLOCK, 1), s_eff, I32))
    sel = jnp.logical_and(causal, jnp.logical_or(gt, jnp.logical_and(eq, s_idx < cut)))
    sel_s[:, keys] = jnp.where(sel, 0.0, -jnp.inf)

    off = pl.multiple_of(qi * Q_BLOCK, Q_BLOCK)
    off_prev = pl.multiple_of(jnp.maximum(qi - 1, 0) * Q_BLOCK, Q_BLOCK)
    ckv = ckv_s[keys, :]
    lats = []
    for h in range(C_HEADS):
        qh = (q_ref[0][:, h * C_LATENT:(h + 1) * C_LATENT] * scale).astype(BF16)
        far = rb_ref[far_bucket, h]
        lg_s[:, keys] = _dot_nt(qh, ckv) + sel_s[:, keys]
        lg_s[:, pl.ds(off, Q_BLOCK)] = (_dot_nt(qh, ckv_s[pl.ds(off, Q_BLOCK), :])
                                        + (bias_s[h][:, 0:Q_BLOCK] - far) + sel_s[:, pl.ds(off, Q_BLOCK)])

        @pl.when(qi > 0)
        def _():
            lg_s[:, pl.ds(off_prev, Q_BLOCK)] = (_dot_nt(qh, ckv_s[pl.ds(off_prev, Q_BLOCK), :])
                                                 + (bias_s[h][:, Q_BLOCK:2 * Q_BLOCK] - far)
                                                 + sel_s[:, pl.ds(off_prev, Q_BLOCK)])

        logits = lg_s[:, keys]
        p = jnp.exp(logits - jnp.max(logits, axis=1, keepdims=True))
        lat = _dot(p.astype(BF16), ckv) / jnp.sum(p, axis=1, keepdims=True)
        lats.append(lat.astype(BF16))
    o_ref[0] = _dot(jnp.concatenate(lats, axis=-1), wuv_ref[...])


def dsa_attention(proj, rel_bias, kv_norm, wuv_bd):
    b, s, _ = proj.shape
    topk = min(DSA_TOPK_MAX, s // 4)
    bk = jnp.asarray(_t5_bucket_table())
    col = lambda name, width: ODD_DST[name] // width
    return pl.pallas_call(
        functools.partial(_dsa_kernel, s_len=s, topk=topk),
        grid=(b, s // Q_BLOCK),
        in_specs=[pl.BlockSpec(memory_space=pltpu.SMEM),
                  pl.BlockSpec(bk.shape, lambda i, j: (0, 0)),
                  pl.BlockSpec((1, Q_BLOCK, 1024), lambda i, j: (i, j, col("q_lat", 1024))),
                  pl.BlockSpec((1, Q_BLOCK, 512), lambda i, j: (i, j, col("iq", 512))),
                  pl.BlockSpec((1, Q_BLOCK, LANES), lambda i, j: (i, j, col("iw", LANES))),
                  pl.BlockSpec((1, s, LANES), lambda i, j: (i, 0, col("c_kv", LANES))),
                  pl.BlockSpec((1, s, LANES), lambda i, j: (i, 0, col("ik", LANES))),
                  pl.BlockSpec((1, C_LATENT), lambda i, j: (0, 0)),
                  pl.BlockSpec(wuv_bd.shape, lambda i, j: (0, 0))],
        out_specs=pl.BlockSpec((1, Q_BLOCK, C_HEADS * C_HEAD_DIM), lambda i, j: (i, j, 0)),
        out_shape=jax.ShapeDtypeStruct((b, s, C_HEADS * C_HEAD_DIM), F32),
        scratch_shapes=[pltpu.VMEM((s, C_LATENT), BF16),
                        pltpu.VMEM((s, IDX_DIM), BF16),
                        pltpu.VMEM((C_HEADS, Q_BLOCK, 2 * Q_BLOCK), F32),
                        pltpu.VMEM((Q_BLOCK, s), I32),
                        pltpu.VMEM((Q_BLOCK, s), F32),
                        pltpu.VMEM((Q_BLOCK, s), F32)],
        compiler_params=_params("parallel", "arbitrary"),
        name="dsa_attention",
    )(rel_bias, bk, proj, proj, proj, proj, proj, kv_norm.reshape(1, C_LATENT), wuv_bd)


def _hgrn_kernel(gam_ref, nrm_ref, q_ref, f_ref, i_ref, g_ref, o_ref, st_ref, *, layer):
    c = pl.program_id(1)
    ch = HG_CHUNK

    @pl.when(c == 0)
    def _():
        st_ref[...] = jnp.zeros(st_ref.shape, F32)

    gam = gam_ref[...]
    pe = jnp.exp(gam - jnp.max(gam, axis=0, keepdims=True))
    pe = pe / jnp.sum(pe, axis=0, keepdims=True)
    lb = jnp.zeros((1, gam.shape[1]), F32)
    for i in range(layer):
        lb = lb + pe[i:i + 1]

    f = lb + (1.0 - lb) * jax.nn.sigmoid(f_ref[0])
    kk = 1.0 - f
    a = jnp.log(f)
    row = lax.broadcasted_iota(I32, (ch, 1), 0)
    sh = 1
    while sh < ch:
        a = a + jnp.where(row >= sh, pltpu.roll(a, sh, axis=0), 0.0)
        sh *= 2
    q = q_ref[0]
    inp = i_ref[0]
    a_end = a[ch - 1:ch, :]
    qe = q * jnp.exp(a)
    kd_end = kk * jnp.exp(a_end - a)
    trow = lax.broadcasted_iota(I32, (HG_SUB, 1), 0)

    outs = []
    for h in range(D_HEADS):
        sl = slice(h * D_KEY, (h + 1) * D_KEY)
        ah, qh, kh, ih = a[:, sl], q[:, sl], kk[:, sl], inp[:, sl]
        ih_bf = ih.astype(BF16)
        st_t = st_ref[h]
        o = _dot_nt(qe[:, sl].astype(BF16), st_t.astype(BF16))
        blocks = []
        for i in range(ch // HG_SUB):
            r0 = i * HG_SUB
            ai, qi, ki, ii = ah[r0:r0 + HG_SUB], qh[r0:r0 + HG_SUB], kh[r0:r0 + HG_SUB], ih[r0:r0 + HG_SUB]
            acc = jnp.zeros((HG_SUB, D_KEY), F32)
            if i > 0:
                r_i = ah[r0 - 1:r0]
                qd = qi * jnp.exp(ai - r_i)
                kd = kh[0:r0] * jnp.exp(r_i - ah[0:r0])
                s_off = _dot_nt(qd.astype(BF16), kd.astype(BF16))
                acc = _dot(s_off.astype(BF16), ih_bf[0:r0])
            for s in range(HG_SUB):
                e = jnp.exp(jnp.where(trow >= s, ai - ai[s:s + 1], -jnp.inf))
                sc = jnp.sum(qi * ki[s:s + 1] * e, axis=1, keepdims=True)
                acc = acc + sc * ii[s:s + 1]
            blocks.append(acc)
        o = o + jnp.concatenate(blocks, axis=0)
        st_ref[h] = st_t * jnp.exp(a_end[:, sl]) + _dot(ih.T.astype(BF16), kd_end[:, sl].astype(BF16))
        outs.append(_rms(o, nrm_ref[:, sl]))
    o_ref[0] = jnp.concatenate(outs, axis=-1) * jax.nn.silu(g_ref[0])


def hgrn2(proj, gamma, hg_norm, layer):
    b, s, _ = proj.shape
    w = D_HEADS * D_KEY
    col = lambda name: ODD_DST[name] // w
    spec = lambda name: pl.BlockSpec((1, HG_CHUNK, w), lambda i, j: (i, j, col(name)))
    return pl.pallas_call(
        functools.partial(_hgrn_kernel, layer=layer),
        grid=(b, s // HG_CHUNK),
        in_specs=[pl.BlockSpec(gamma.shape, lambda i, j: (0, 0)),
                  pl.BlockSpec((1, w), lambda i, j: (0, 0)),
                  spec("hq"), spec("hf"), spec("hi"), spec("hg")],
        out_specs=pl.BlockSpec((1, HG_CHUNK, w), lambda i, j: (i, j, 0)),
        out_shape=jax.ShapeDtypeStruct((b, s, w), F32),
        scratch_shapes=[pltpu.VMEM((D_HEADS, D_KEY, D_KEY), F32)],
        compiler_params=_params("parallel", "arbitrary"),
        name="hgrn2",
    )(gamma, hg_norm.reshape(1, w), proj, proj, proj, proj)


def _outproj_kernel(h_ref, a_ref, b_ref, w_ref, o_ref):
    ka = a_ref.shape[1]
    o_ref[...] = (h_ref[...] + _dot(a_ref[...].astype(BF16), w_ref[0:ka, :])
                  + _dot(b_ref[...].astype(BF16), w_ref[ka:, :]))


def outproj_residual(h, a, b, w, tm=512):
    m, d = h.shape
    return pl.pallas_call(
        _outproj_kernel,
        grid=(m // tm,),
        in_specs=[pl.BlockSpec((tm, d), lambda i: (i, 0)),
                  pl.BlockSpec((tm, a.shape[1]), lambda i: (i, 0)),
                  pl.BlockSpec((tm, b.shape[1]), lambda i: (i, 0)),
                  pl.BlockSpec(w.shape, lambda i: (0, 0))],
        out_specs=pl.BlockSpec((tm, d), lambda i: (i, 0)),
        out_shape=jax.ShapeDtypeStruct((m, d), F32),
        compiler_params=_params("parallel"),
        name="outproj_residual",
    )(h, a, b, w)


def _xattn_kernel(h_ref, g_ref, wq_ref, kv_ref, wo_ref, o_ref):
    x = h_ref[0]
    q = _dot(_rms(x, g_ref[...]).astype(BF16), wq_ref[...])
    kv = kv_ref[0]
    dm = X_HEADS * X_HEAD_DIM
    outs = []
    for h in range(X_HEADS):
        sl = slice(h * X_HEAD_DIM, (h + 1) * X_HEAD_DIM)
        logits = _dot_nt(q[:, sl].astype(BF16), kv[:, sl]) * (X_HEAD_DIM ** -0.5)
        p = jnp.exp(logits - jnp.max(logits, axis=1, keepdims=True))
        p = p / jnp.sum(p, axis=1, keepdims=True)
        outs.append(_dot(p.astype(BF16), kv[:, dm + h * X_HEAD_DIM:dm + (h + 1) * X_HEAD_DIM]).astype(BF16))
    o_ref[0] = x + _dot(jnp.concatenate(outs, axis=-1), wo_ref[...])


def memory_cross_attn(h, g, wq, kv, wo, ts=512):
    b, s, d = h.shape
    return pl.pallas_call(
        _xattn_kernel,
        grid=(b, s // ts),
        in_specs=[pl.BlockSpec((1, ts, d), lambda i, j: (i, j, 0)),
                  pl.BlockSpec((1, d), lambda i, j: (0, 0)),
                  pl.BlockSpec(wq.shape, lambda i, j: (0, 0)),
                  pl.BlockSpec((1,) + kv.shape[1:], lambda i, j: (i, 0, 0)),
                  pl.BlockSpec(wo.shape, lambda i, j: (0, 0))],
        out_specs=pl.BlockSpec((1, ts, d), lambda i, j: (i, j, 0)),
        out_shape=jax.ShapeDtypeStruct(h.shape, F32),
        compiler_params=_params("parallel", "parallel"),
        name="memory_cross_attn",
    )(h, g.reshape(1, d), wq, kv, wo)


def _extract_top(work, iota, rounds):
    rows = work.shape[0]
    rank = jnp.full(work.shape, NO_RANK, I32)
    vals = jnp.zeros((rounds, work.shape[1]), F32)
    riota = lax.broadcasted_iota(I32, (rounds, 1), 0)
    for r in range(rounds):
        m = jnp.max(work, axis=0, keepdims=True)
        first = jnp.min(jnp.where(work == m, iota, rows), axis=0, keepdims=True)
        hit = iota == first
        rank = jnp.where(hit, r, rank)
        work = jnp.where(hit, -jnp.inf, work)
        vals = jnp.where(riota == r, m, vals)
    return vals, rank


def _batcher_pairs(n):
    pairs = []
    p = 1
    while p < n:
        k = p
        while k >= 1:
            for j in range(k % p, n - k, 2 * k):
                for i in range(min(k, n - j - k)):
                    if (i + j) // (2 * p) == (i + j + k) // (2 * p):
                        pairs.append((i + j, i + j + k))
            k //= 2
        p *= 2
    return pairs


_SORT16 = _batcher_pairs(PEER_TOPK)


def _compare_exchange(w, i, j):
    a, b = w[i], w[j]
    if b is None:
        return
    if a is None:
        w[i], w[j] = b, None
        return
    w[i], w[j] = jnp.maximum(a, b), jnp.minimum(a, b)


def _sorted_top(slabs):
    k = PEER_TOPK
    w = list(slabs) + [None] * (k - len(slabs))
    for i, j in _SORT16:
        _compare_exchange(w, i, j)
    for shift in (4, 2, 1):
        r = [None if x is None else pltpu.roll(x, shift, axis=0) for x in w]
        t = []
        for i in range(k):
            a, b = w[i], r[k - 1 - i]
            t.append(b if a is None else a if b is None else jnp.maximum(a, b))
        w = t
        d = k // 2
        while d >= 1:
            for i in range(k):
                if i & d == 0:
                    _compare_exchange(w, i, i + d)
            d //= 2
    return w


def _slabs(x):
    return [x[8 * j:8 * j + 8] for j in range(x.shape[0] // 8)]


def _count_ge(slabs, thr):
    acc = None
    for x in slabs:
        c = jnp.where(x >= thr, 1.0, 0.0)
        acc = c if acc is None else acc + c
    return jnp.sum(acc, axis=0, keepdims=True)


def _route_head_fast(s1, s2, iota8):
    k = PEER_TOPK
    c1, c2 = _slabs(s1), _slabs(s2)
    a, b = _sorted_top(c1), _sorted_top(c2)
    tie = jnp.logical_or(_count_ge(c1, a[k - 1]) != k, _count_ge(c2, b[k - 1]) != k)
    for i in range(k - 1):
        tie = jnp.logical_or(tie, jnp.logical_or(a[i][0:1] <= a[i + 1][0:1], b[i][0:1] <= b[i + 1][0:1]))

    def stack8(vals):
        out = vals[0]
        for r in range(1, 8):
            out = jnp.where(iota8 == r, vals[r], out)
        return out

    a_hi, b_lo, b_hi = stack8(a[8:]), stack8(b[:8]), stack8(b[8:])
    cand = [a[0] + b_lo, a[0] + b_hi]
    for i in range(1, 8):
        cand.append(jnp.where(iota8 < k // (i + 1), a[i] + b_lo, -jnp.inf))
    cand.append(a_hi + b[0])
    thr = _sorted_top(cand)[k - 1]
    tie = jnp.logical_or(tie, _count_ge(cand, thr) != k)
    chosen = [jnp.where(x >= thr, 1.0, 0.0) for x in cand]
    top = a[0] + b[0]
    z = None
    for x, m in zip(cand, chosen):
        e = m * jnp.exp(x - top)
        z = e if z is None else z + e
    z = jnp.sum(z, axis=0, keepdims=True)
    n = [jnp.sum(chosen[0] + chosen[1], axis=0, keepdims=True)]
    n += [jnp.sum(chosen[i + 1], axis=0, keepdims=True) for i in range(1, 8)]
    n += [chosen[9][r:r + 1] for r in range(8)]
    n = [jnp.broadcast_to(x, (8, x.shape[1])) for x in n]
    n1, r2 = [], []
    for x1, x2 in zip(c1, c2):
        acc_n = jnp.zeros(x1.shape, F32)
        acc_r = jnp.full(x2.shape, float(NO_RANK), F32)
        for r in range(k - 1, -1, -1):
            acc_n = jnp.where(x1 >= a[r], n[r], acc_n)
            acc_r = jnp.where(x2 >= b[r], float(r), acc_r)
        n1.append(acc_n)
        r2.append(acc_r)
    a_tab = jnp.exp(s1 - a[0][0:1]) / z
    b_tab = jnp.exp(s2 - b[0][0:1])
    return a_tab, jnp.concatenate(n1, axis=0), b_tab, jnp.concatenate(r2, axis=0), tie


def _route_head_exact(s1, s2, iota_keys, iota8, iota_cand):
    k = PEER_TOPK
    tt = s1.shape[1]
    n_full = 8
    cand_rows = k + 8 * (n_full - 1) + 8
    a, rank1 = _extract_top(s1, iota_keys, k)
    b, rank2 = _extract_top(s2, iota_keys, k)
    blocks = [a[0:1] + b]
    for i in range(1, n_full):
        blocks.append(jnp.where(iota8 < k // (i + 1), a[i:i + 1] + b[0:8], -jnp.inf))
    blocks.append(a[8:16] + b[0:1])
    cand = jnp.concatenate(blocks, axis=0)
    _, crank = _extract_top(cand, iota_cand, k)
    chosen = crank < k
    z = jnp.sum(jnp.where(chosen, jnp.exp(cand - (a[0:1] + b[0:1])), 0.0), axis=0, keepdims=True)
    chosen_f = jnp.where(chosen, 1.0, 0.0)
    n_lo = jnp.zeros((8, tt), F32)
    n_lo = jnp.where(iota8 == 0, jnp.sum(chosen_f[0:k], axis=0, keepdims=True), n_lo)
    for i in range(1, n_full):
        r0 = k + 8 * (i - 1)
        n_lo = jnp.where(iota8 == i, jnp.sum(chosen_f[r0:r0 + 8], axis=0, keepdims=True), n_lo)
    n_hi = chosen_f[cand_rows - 8:cand_rows]
    n1 = jnp.zeros((N_KEYS, tt), F32)
    for r in range(k):
        n_r = n_lo[r:r + 1] if r < 8 else n_hi[r - 8:r - 7]
        n1 = jnp.where(rank1 == r, n_r, n1)
    return jnp.exp(s1 - a[0:1]) / z, n1, jnp.exp(s2 - b[0:1]), rank2.astype(F32)


def _route_kernel(h_ref, g_ref, wq_ref, keys_ref, hn_ref, a_ref, n1_ref, b_ref, r2_ref, q_s, *, tt):
    hn = _rms(h_ref[...], g_ref[...])
    hn_ref[...] = hn.T.astype(BF16)
    q_s[...] = _dot(hn.astype(BF16), wq_ref[...]).astype(BF16)
    iota8 = lax.broadcasted_iota(I32, (8, 1), 0)

    def scores(h):
        s1 = _dot_nt(keys_ref[h, 0], q_s[:, (2 * h) * N_KEYS:(2 * h + 1) * N_KEYS])
        s2 = _dot_nt(keys_ref[h, 1], q_s[:, (2 * h + 1) * N_KEYS:(2 * h + 2) * N_KEYS])
        return s1, s2

    def store(h, tabs):
        a_ref[h], n1_ref[h] = tabs[0], tabs[1]
        b_ref[h], r2_ref[h] = tabs[2].astype(BF16), tabs[3].astype(BF16)

    tie = jnp.zeros((1, tt), jnp.bool_)
    for h in range(PEER_HEADS):
        tabs = _route_head_fast(*scores(h), iota8)
        store(h, tabs[:4])
        tie = jnp.logical_or(tie, tabs[4])

    @pl.when(jnp.max(jnp.where(tie, 1.0, 0.0)) > 0.0)
    def _():
        iota_keys = lax.broadcasted_iota(I32, (N_KEYS, tt), 0)
        iota_cand = lax.broadcasted_iota(I32, (PEER_TOPK + 8 * 8, tt), 0)
        for h in range(PEER_HEADS):
            store(h, _route_head_exact(*scores(h), iota_keys, iota8, iota_cand))


def peer_route(h2d, g, wq, keys, tt=128):
    t, d = h2d.shape
    tab = jax.ShapeDtypeStruct((PEER_HEADS, N_KEYS, t), F32)
    tab_bf = jax.ShapeDtypeStruct((PEER_HEADS, N_KEYS, t), BF16)
    tab_spec = pl.BlockSpec((PEER_HEADS, N_KEYS, tt), lambda i: (0, 0, i))
    return pl.pallas_call(
        functools.partial(_route_kernel, tt=tt),
        grid=(t // tt,),
        in_specs=[pl.BlockSpec((tt, d), lambda i: (i, 0)),
                  pl.BlockSpec((1, d), lambda i: (0, 0)),
                  pl.BlockSpec(wq.shape, lambda i: (0, 0)),
                  pl.BlockSpec(keys.shape, lambda i: (0, 0, 0, 0))],
        out_specs=[pl.BlockSpec((d, tt), lambda i: (0, i)), tab_spec, tab_spec, tab_spec, tab_spec],
        out_shape=[jax.ShapeDtypeStruct((d, t), BF16), tab, tab, tab_bf, tab_bf],
        scratch_shapes=[pltpu.VMEM((tt, wq.shape[1]), BF16)],
        compiler_params=_params("parallel"),
        name="peer_route",
    )(h2d, g.reshape(1, d), wq, keys)


def _peer_dense_kernel(h_ref, hn_ref, a_ref, n1_ref, b_ref, r2_ref, u_ref, vt_ref, fg_ref, o_ref, acc_ref, p0_ref, p1_ref,
                       *, te, final_norm):
    e = pl.program_id(1)
    n_tiles = pl.num_programs(1) - 1

    @pl.when(e == 0)
    def _():
        acc_ref[...] = jnp.zeros(acc_ref.shape, F32)
        p1_ref[...] = jnp.zeros(p1_ref.shape, BF16)

    hn = hn_ref[...]
    tt = hn.shape[1]
    keys_per_sub = PEER_SUB // N_KEYS
    tile = jnp.minimum(e, n_tiles - 1)

    def step(p_cur, p_prev):
        n_sub = te // PEER_SUB
        d_rows = acc_ref.shape[0] // n_sub

        def activations(sb):
            r0 = sb * PEER_SUB
            p_cur[r0:r0 + PEER_SUB, :] = jax.nn.gelu(_dot(u_ref[r0:r0 + PEER_SUB, :], hn).astype(BF16))

        activations(0)
        for sb in range(n_sub):
            if sb + 1 < n_sub:
                activations(sb + 1)
            out_rows = slice(sb * d_rows, (sb + 1) * d_rows)
            acc_ref[out_rows, :] += _dot(vt_ref[out_rows, :], p_prev[...])
            r0 = sb * PEER_SUB
            for kb in range(keys_per_sub):
                i1 = tile * (te // N_KEYS) + sb * keys_per_sub + kb
                rows = slice(r0 + kb * N_KEYS, r0 + (kb + 1) * N_KEYS)
                a_full = [a_ref[h, pl.ds(i1, 1), :].astype(BF16) for h in range(PEER_HEADS)]
                n_full = [n1_ref[h, pl.ds(i1, 1), :].astype(BF16) for h in range(PEER_HEADS)]
                for c in range(tt // PEER_COLS):
                    cols = slice(c * PEER_COLS, (c + 1) * PEER_COLS)
                    gate = jnp.zeros((N_KEYS, PEER_COLS), BF16)
                    for h in range(PEER_HEADS):
                        gate = gate + jnp.where(r2_ref[h, :, cols] < n_full[h][:, cols],
                                                b_ref[h, :, cols] * a_full[h][:, cols], jnp.zeros((), BF16))
                    p_cur[rows, cols] = gate * p_cur[rows, cols]

    @pl.when(e % 2 == 0)
    def _():
        step(p0_ref, p1_ref)

    @pl.when(e % 2 == 1)
    def _():
        step(p1_ref, p0_ref)

    @pl.when(e == n_tiles)
    def _():
        out = h_ref[...] + acc_ref[...].T
        if final_norm:
            out = _rms(out, fg_ref[...])
        o_ref[...] = out


def peer_dense(h2d, hn_t, a, n1, b, r2, u, vt, final_g, final_norm, tt=512, te=1024):
    t, d = h2d.shape
    n_tiles = u.shape[0] // te
    tab_spec = pl.BlockSpec((PEER_HEADS, N_KEYS, tt), lambda i, j: (0, 0, i))
    return pl.pallas_call(
        functools.partial(_peer_dense_kernel, te=te, final_norm=final_norm),
        grid=(t // tt, n_tiles + 1),
        in_specs=[pl.BlockSpec((tt, d), lambda i, j: (i, 0)),
                  pl.BlockSpec((d, tt), lambda i, j: (0, i)),
                  tab_spec, tab_spec, tab_spec, tab_spec,
                  pl.BlockSpec((te, d), lambda i, j: (jnp.minimum(j, n_tiles - 1), 0)),
                  pl.BlockSpec((d, te), lambda i, j: (0, jnp.maximum(j - 1, 0))),
                  pl.BlockSpec((1, d), lambda i, j: (0, 0))],
        out_specs=pl.BlockSpec((tt, d), lambda i, j: (i, 0)),
        out_shape=jax.ShapeDtypeStruct((t, d), F32),
        scratch_shapes=[pltpu.VMEM((d, tt), F32), pltpu.VMEM((te, tt), BF16), pltpu.VMEM((te, tt), BF16)],
        compiler_params=_params("parallel", "arbitrary"),
        name="peer_dense",
    )(h2d, hn_t, a, n1, b, r2, u, vt, final_g.reshape(1, d))


def _pad_odd_w_in(w):
    out = jnp.zeros((w.shape[0], ODD_COLS), w.dtype)
    for name, (src, width) in ODD_SRC.items():
        out = lax.dynamic_update_slice(out, w[:, src:src + width], (0, ODD_DST[name]))
    return out


def _block_diag_uv(w_uv):
    out = jnp.zeros((C_HEADS * C_LATENT, C_HEADS * C_HEAD_DIM), w_uv.dtype)
    for h in range(C_HEADS):
        out = lax.dynamic_update_slice(out, w_uv[h], (h * C_LATENT, h * C_HEAD_DIM))
    return out


def kernel(x, mem, mix_norm, even_w_in, even_conv_w, even_pool_w, even_pool_scale, even_w_out, odd_w_in, odd_kv_norm, odd_w_uv, odd_hg_norm, odd_w_out, hgrn_gamma, rel_bias, mem_norm, xattn_norm, xattn_wq, xattn_wkv, xattn_wo, peer_norm, peer_wq, peer_keys, peer_u, peer_v, final_norm):
    bsz, s, d = x.shape
    t = bsz * s
    depth = mix_norm.shape[0]
    mem2d = mem.reshape(-1, d)
    h = x
    for l in range(depth):
        j = l // 2
        if l % 2 == 0:
            h = even_mixer(h, mix_norm[l], even_w_in[j].astype(BF16), even_conv_w[j].reshape(3, A_WIDTH),
                           even_pool_w[j].astype(BF16), even_pool_scale[j].reshape(1, -1),
                           even_w_out[j].astype(BF16))
        else:
            proj = norm_matmul(h.reshape(t, d), mix_norm[l], _pad_odd_w_in(odd_w_in[j]).astype(BF16), F32)
            proj = proj.reshape(bsz, s, ODD_COLS)
            y_c = dsa_attention(proj, rel_bias, odd_kv_norm[j], _block_diag_uv(odd_w_uv[j]).astype(BF16))
            y_d = hgrn2(proj, hgrn_gamma, odd_hg_norm[j], l)
            h = outproj_residual(h.reshape(t, d), y_c.reshape(t, -1), y_d.reshape(t, -1),
                                 odd_w_out[j].astype(BF16)).reshape(bsz, s, d)
        kv = norm_matmul(mem2d, mem_norm, xattn_wkv[l].astype(BF16), BF16).reshape(bsz, -1, 2 * d)
        h = memory_cross_attn(h, xattn_norm[l], xattn_wq[l].astype(BF16), kv, xattn_wo[l].astype(BF16))
        h2d = h.reshape(t, d)
        hn, a, n1, b, r2 = peer_route(h2d, peer_norm[l], peer_wq[l].astype(BF16), peer_keys[l].astype(BF16))
        h = peer_dense(h2d, hn, a, n1, b, r2, peer_u[l].astype(BF16), peer_v[l].T.astype(BF16),
                       final_norm, l == depth - 1).reshape(bsz, s, d)
    return h
```

```python
import functools
import math

import numpy as np
import jax
import jax.numpy as jnp
from jax import lax
from jax.experimental import pallas as pl
from jax.experimental.pallas import tpu as pltpu

F32 = jnp.float32
BF16 = jnp.bfloat16
I32 = jnp.int32

EPS = 1e-6
LANES = 128
INT_MIN = -(2 ** 31)
VMEM_LIMIT = 56 * 1024 * 1024

POOL_WINDOWS = (2, 4, 8, 16)
HALO = 16
A_WIDTH = 512
POOL_GROUP = 128

C_HEADS = 8
C_LATENT = 128
C_HEAD_DIM = 64
IDX_HEADS = 8
IDX_DIM = 64
DSA_TOPK_MAX = 256
Q_BLOCK = 128
DSA_TIERS = 4
REL_BUCKETS = 32
REL_MAX_DIST = 128

D_HEADS = 4
D_KEY = 128
HG_CHUNK = 128
HG_SUB = 16

X_HEADS = 4
X_HEAD_DIM = 256

N_KEYS = 128
PEER_HEADS = 8
PEER_TOPK = 16
NO_RANK = 127
PEER_SUB = 256
PEER_COLS = 256

ODD_COLS = 4096
ODD_SRC = dict(q_lat=(0, 1024), c_kv=(1024, 128), iq=(1152, 512), ik=(1664, 64), iw=(1728, 8),
               hq=(1736, 512), hf=(2248, 512), hi=(2760, 512), hg=(3272, 512))
ODD_DST = dict(q_lat=0, iq=1024, c_kv=1536, ik=1664, iw=1792, hq=2048, hf=2560, hi=3072, hg=3584)

NT_DIMS = (((1,), (1,)), ((), ()))


def _params(*sem):
    return pltpu.CompilerParams(dimension_semantics=sem, vmem_limit_bytes=VMEM_LIMIT)


def _rms(x, g):
    return x * lax.rsqrt(jnp.mean(x * x, axis=-1, keepdims=True) + EPS) * g


def _dot(a, b):
    return jnp.dot(a, b, preferred_element_type=F32)


def _dot_nt(a, b):
    return lax.dot_general(a, b, NT_DIMS, preferred_element_type=F32)


def _norm_matmul_kernel(x_ref, g_ref, w_ref, o_ref):
    y = _rms(x_ref[...], g_ref[...])
    o_ref[...] = _dot(y.astype(BF16), w_ref[...]).astype(o_ref.dtype)


def norm_matmul(x, g, w, out_dtype, tm=256):
    m, d = x.shape
    n = w.shape[1]
    return pl.pallas_call(
        _norm_matmul_kernel,
        grid=(m // tm,),
        in_specs=[pl.BlockSpec((tm, d), lambda i: (i, 0)),
                  pl.BlockSpec((1, d), lambda i: (0, 0)),
                  pl.BlockSpec((d, n), lambda i: (0, 0))],
        out_specs=pl.BlockSpec((tm, n), lambda i: (i, 0)),
        out_shape=jax.ShapeDtypeStruct((m, n), out_dtype),
        compiler_params=_params("parallel"),
        name="norm_matmul",
    )(x, g.reshape(1, d), w)


def _even_kernel(x_ref, g_ref, win_ref, cw_ref, pw_ref, ps_ref, wout_ref, o_ref, vbuf, pbuf, *, ts):
    j = pl.program_id(1)

    @pl.when(j == 0)
    def _():
        vbuf[0:HALO, :] = jnp.zeros((HALO, A_WIDTH), F32)
        pbuf[0:HALO, :] = jnp.zeros((HALO, A_WIDTH), F32)

    x = x_ref[0]
    proj = _dot(_rms(x, g_ref[...]).astype(BF16), win_ref[...])
    u = proj[:, 0:512]
    gate_c = proj[:, 512:1024]
    gate_b = proj[:, 1024:1536]
    pv = proj[:, 1536:2048]
    vbuf[HALO:HALO + ts, :] = gate_c * u
    pbuf[HALO:HALO + ts, :] = pv

    cw = cw_ref[...]
    z = (cw[2:3] * vbuf[HALO:HALO + ts, :] + cw[1:2] * vbuf[HALO - 1:HALO - 1 + ts, :]
         + cw[0:1] * vbuf[HALO - 2:HALO - 2 + ts, :])
    y_a = gate_b * z

    pos = j * ts + lax.broadcasted_iota(I32, (ts, 1), 0)
    y_b = []
    for gi, w in enumerate(POOL_WINDOWS):
        c0 = gi * POOL_GROUP
        acc = pbuf[HALO:HALO + ts, c0:c0 + POOL_GROUP]
        for k in range(1, w):
            acc = acc + pbuf[HALO - k:HALO - k + ts, c0:c0 + POOL_GROUP]
        cnt = jnp.minimum(pos + 1, w).astype(F32)
        pooled = acc / cnt - pv[:, c0:c0 + POOL_GROUP]
        y_b.append(_dot(pooled.astype(BF16), pw_ref[gi]))
    y_b = jnp.concatenate(y_b, axis=-1) * ps_ref[...]

    o_ref[0] = (x + _dot(y_a.astype(BF16), wout_ref[0:A_WIDTH, :])
                + _dot(y_b.astype(BF16), wout_ref[A_WIDTH:2 * A_WIDTH, :]))

    vbuf[0:HALO, :] = vbuf[ts:ts + HALO, :]
    pbuf[0:HALO, :] = pbuf[ts:ts + HALO, :]


def even_mixer(h, g, w_in, conv_w, pool_w, pool_scale, w_out, ts=512):
    b, s, d = h.shape
    return pl.pallas_call(
        functools.partial(_even_kernel, ts=ts),
        grid=(b, s // ts),
        in_specs=[pl.BlockSpec((1, ts, d), lambda i, j: (i, j, 0)),
                  pl.BlockSpec((1, d), lambda i, j: (0, 0)),
                  pl.BlockSpec(w_in.shape, lambda i, j: (0, 0)),
                  pl.BlockSpec(conv_w.shape, lambda i, j: (0, 0)),
                  pl.BlockSpec(pool_w.shape, lambda i, j: (0, 0, 0)),
                  pl.BlockSpec(pool_scale.shape, lambda i, j: (0, 0)),
                  pl.BlockSpec(w_out.shape, lambda i, j: (0, 0))],
        out_specs=pl.BlockSpec((1, ts, d), lambda i, j: (i, j, 0)),
        out_shape=jax.ShapeDtypeStruct(h.shape, F32),
        scratch_shapes=[pltpu.VMEM((HALO + ts, A_WIDTH), F32), pltpu.VMEM((HALO + ts, A_WIDTH), F32)],
        compiler_params=_params("parallel", "arbitrary"),
        name="even_mixer",
    )(h, g.reshape(1, d), w_in, conv_w, pool_w, pool_scale, w_out)


def _t5_bucket_table():
    t = np.arange(Q_BLOCK)[:, None]
    s = np.arange(Q_BLOCK)[None, :]
    dist = np.concatenate([t - s, t - s + Q_BLOCK], axis=1)
    max_exact = REL_BUCKETS // 2
    d = np.maximum(dist, 0)
    log_ratio = (np.log(np.maximum(d, 1).astype(np.float32) / np.float32(max_exact))
                 / np.float32(math.log(REL_MAX_DIST / max_exact)))
    large = max_exact + (log_ratio * np.float32(REL_BUCKETS - max_exact)).astype(np.int32)
    large = np.minimum(large, REL_BUCKETS - 1)
    return np.where(d < max_exact, d, large).astype(np.int32)


def _dsa_kernel(rb_ref, bk_ref, q_ref, iq_ref, iw_ref, ckv_ref, ik_ref, kvn_ref, wuv_ref, o_ref,
                ckv_s, ik_s, bias_s, key_s, sel_s, lg_s, *, s_len, topk):
    qi = pl.program_id(1)

    @pl.when(qi == 0)
    def _():
        ckv_s[...] = _rms(ckv_ref[0], kvn_ref[...]).astype(BF16)
        ik_s[...] = ik_ref[0][:, 0:IDX_DIM].astype(BF16)
        bk = bk_ref[...]
        for h in range(C_HEADS):
            tile = jnp.zeros(bk.shape, F32)
            for k in range(REL_BUCKETS):
                tile = jnp.where(bk == k, rb_ref[k, h], tile)
            bias_s[h] = tile

    n_q = s_len // Q_BLOCK
    tiers = DSA_TIERS if n_q % DSA_TIERS == 0 else 1
    for tier in range(tiers):
        lo, hi = tier * n_q // tiers, (tier + 1) * n_q // tiers

        @pl.when(jnp.logical_and(qi >= lo, qi < hi))
        def _(s_eff=hi * Q_BLOCK):
            _dsa_block(rb_ref, q_ref, iq_ref, iw_ref, wuv_ref, o_ref, ckv_s, ik_s, bias_s, key_s, sel_s, lg_s,
                       qi=qi, s_eff=s_eff, topk=topk)


def _dsa_block(rb_ref, q_ref, iq_ref, iw_ref, wuv_ref, o_ref, ckv_s, ik_s, bias_s, key_s, sel_s, lg_s, *, qi, s_eff, topk):
    scale = C_LATENT ** -0.5
    far_bucket = REL_BUCKETS - 1
    t_idx = qi * Q_BLOCK + lax.broadcasted_iota(I32, (Q_BLOCK, 1), 0)
    s_idx = lax.broadcasted_iota(I32, (1, s_eff), 1)
    causal = s_idx <= t_idx
    keys = slice(0, s_eff)

    iq = iq_ref[0]
    iw = iw_ref[0] * (IDX_HEADS ** -0.5 * IDX_DIM ** -0.5)
    ik = ik_s[keys, :]
    score = jnp.zeros((Q_BLOCK, s_eff), F32)
    for h in range(IDX_HEADS):
        s_h = _dot_nt(iq[:, h * IDX_DIM:(h + 1) * IDX_DIM].astype(BF16), ik)
        score = score + jnp.maximum(s_h, 0.0) * iw[:, h:h + 1]

    bits = lax.bitcast_convert_type(score + 0.0, I32)
    key = jnp.where(bits < 0, bits ^ jnp.int32(0x7FFFFFFF), bits)
    key_s[:, keys] = jnp.where(causal, key, jnp.int32(INT_MIN))

    def count(mask):
        return jnp.sum(jnp.where(mask, 1.0, 0.0), axis=1, keepdims=True)

    def thr_step(i, ans):
        cand = ans + lax.shift_left(jnp.int32(1), 31 - i)
        return jnp.where(count(key_s[:, keys] >= cand) >= topk, cand, ans)

    thr = lax.fori_loop(0, 32, thr_step, jnp.full((Q_BLOCK, 1), INT_MIN, I32))
    key = key_s[:, keys]
    gt = key > thr
    eq = key == thr
    need = topk - count(gt)
    tie_rows = jnp.logical_and(count(eq) > need, thr != jnp.int32(INT_MIN))
    any_tie = jnp.max(jnp.where(tie_rows, 1.0, 0.0)) > 0.0
    cut_bits = s_eff.bit_length()

    def tie_cut():
        def step(i, cut):
            cand = cut + lax.shift_left(jnp.int32(1), cut_bits - 1 - i)
            c = count(jnp.logical_and(key_s[:, keys] == thr, s_idx < cand))
            return jnp.where(c <= need, cand, cut)
        return lax.fori_loop(0, cut_bits, step, jnp.zeros((Q_BLOCK, 1), I32))

    cut = lax.cond(any_tie, tie_cut, lambda: jnp.full((Q_BLOCK, 1), s_eff, I32))
    sel = jnp.logical_and(causal, jnp.logical_or(gt, jnp.logical_and(eq, s_idx < cut)))
    sel_s[:, keys] = jnp.where(sel, 0.0, -jnp.inf)

    off = pl.multiple_of(qi * Q_BLOCK, Q_BLOCK)
    off_prev = pl.multiple_of(jnp.maximum(qi - 1, 0) * Q_BLOCK, Q_BLOCK)
    ckv = ckv_s[keys, :]
    lats = []
    for h in range(C_HEADS):
        qh = (q_ref[0][:, h * C_LATENT:(h + 1) * C_LATENT] * scale).astype(BF16)
        far = rb_ref[far_bucket, h]
        lg_s[:, keys] = _dot_nt(qh, ckv) + sel_s[:, keys]
        lg_s[:, pl.ds(off, Q_BLOCK)] = (_dot_nt(qh, ckv_s[pl.ds(off, Q_BLOCK), :])
                                        + (bias_s[h][:, 0:Q_BLOCK] - far) + sel_s[:, pl.ds(off, Q_BLOCK)])

        @pl.when(qi > 0)
        def _():
            lg_s[:, pl.ds(off_prev, Q_BLOCK)] = (_dot_nt(qh, ckv_s[pl.ds(off_prev, Q_BLOCK), :])
                                                 + (bias_s[h][:, Q_BLOCK:2 * Q_BLOCK] - far)
                                                 + sel_s[:, pl.ds(off_prev, Q_BLOCK)])

        logits = lg_s[:, keys]
        p = jnp.exp(logits - jnp.max(logits, axis=1, keepdims=True))
        lat = _dot(p.astype(BF16), ckv) / jnp.sum(p, axis=1, keepdims=True)
        lats.append(lat.astype(BF16))
    o_ref[0] = _dot(jnp.concatenate(lats, axis=-1), wuv_ref[...])


def dsa_attention(proj, rel_bias, kv_norm, wuv_bd):
    b, s, _ = proj.shape
    topk = min(DSA_TOPK_MAX, s // 4)
    bk = jnp.asarray(_t5_bucket_table())
    col = lambda name, width: ODD_DST[name] // width
    return pl.pallas_call(
        functools.partial(_dsa_kernel, s_len=s, topk=topk),
        grid=(b, s // Q_BLOCK),
        in_specs=[pl.BlockSpec(memory_space=pltpu.SMEM),
                  pl.BlockSpec(bk.shape, lambda i, j: (0, 0)),
                  pl.BlockSpec((1, Q_BLOCK, 1024), lambda i, j: (i, j, col("q_lat", 1024))),
                  pl.BlockSpec((1, Q_BLOCK, 512), lambda i, j: (i, j, col("iq", 512))),
                  pl.BlockSpec((1, Q_BLOCK, LANES), lambda i, j: (i, j, col("iw", LANES))),
                  pl.BlockSpec((1, s, LANES), lambda i, j: (i, 0, col("c_kv", LANES))),
                  pl.BlockSpec((1, s, LANES), lambda i, j: (i, 0, col("ik", LANES))),
                  pl.BlockSpec((1, C_LATENT), lambda i, j: (0, 0)),
                  pl.BlockSpec(wuv_bd.shape, lambda i, j: (0, 0))],
        out_specs=pl.BlockSpec((1, Q_BLOCK, C_HEADS * C_HEAD_DIM), lambda i, j: (i, j, 0)),
        out_shape=jax.ShapeDtypeStruct((b, s, C_HEADS * C_HEAD_DIM), F32),
        scratch_shapes=[pltpu.VMEM((s, C_LATENT), BF16),
                        pltpu.VMEM((s, IDX_DIM), BF16),
                        pltpu.VMEM((C_HEADS, Q_BLOCK, 2 * Q_BLOCK), F32),
                        pltpu.VMEM((Q_BLOCK, s), I32),
                        pltpu.VMEM((Q_BLOCK, s), F32),
                        pltpu.VMEM((Q_BLOCK, s), F32)],
        compiler_params=_params("parallel", "arbitrary"),
        name="dsa_attention",
    )(rel_bias, bk, proj, proj, proj, proj, proj, kv_norm.reshape(1, C_LATENT), wuv_bd)


def _hgrn_kernel(gam_ref, nrm_ref, q_ref, f_ref, i_ref, g_ref, o_ref, st_ref, *, layer):
    c = pl.program_id(1)
    ch = HG_CHUNK

    @pl.when(c == 0)
    def _():
        st_ref[...] = jnp.zeros(st_ref.shape, F32)

    gam = gam_ref[...]
    pe = jnp.exp(gam - jnp.max(gam, axis=0, keepdims=True))
    pe = pe / jnp.sum(pe, axis=0, keepdims=True)
    lb = jnp.zeros((1, gam.shape[1]), F32)
    for i in range(layer):
        lb = lb + pe[i:i + 1]

    f = lb + (1.0 - lb) * jax.nn.sigmoid(f_ref[0])
    kk = 1.0 - f
    a = jnp.log(f)
    row = lax.broadcasted_iota(I32, (ch, 1), 0)
    sh = 1
    while sh < ch:
        a = a + jnp.where(row >= sh, pltpu.roll(a, sh, axis=0), 0.0)
        sh *= 2
    q = q_ref[0]
    inp = i_ref[0]
    a_end = a[ch - 1:ch, :]
    qe = q * jnp.exp(a)
    kd_end = kk * jnp.exp(a_end - a)
    trow = lax.broadcasted_iota(I32, (HG_SUB, 1), 0)

    outs = []
    for h in range(D_HEADS):
        sl = slice(h * D_KEY, (h + 1) * D_KEY)
        ah, qh, kh, ih = a[:, sl], q[:, sl], kk[:, sl], inp[:, sl]
        ih_bf = ih.astype(BF16)
        st_t = st_ref[h]
        o = _dot_nt(qe[:, sl].astype(BF16), st_t.astype(BF16))
        blocks = []
        for i in range(ch // HG_SUB):
            r0 = i * HG_SUB
            ai, qi, ki, ii = ah[r0:r0 + HG_SUB], qh[r0:r0 + HG_SUB], kh[r0:r0 + HG_SUB], ih[r0:r0 + HG_SUB]
            acc = jnp.zeros((HG_SUB, D_KEY), F32)
            if i > 0:
                r_i = ah[r0 - 1:r0]
                qd = qi * jnp.exp(ai - r_i)
                kd = kh[0:r0] * jnp.exp(r_i - ah[0:r0])
                s_off = _dot_nt(qd.astype(BF16), kd.astype(BF16))
                acc = _dot(s_off.astype(BF16), ih_bf[0:r0])
            for s in range(HG_SUB):
                e = jnp.exp(jnp.where(trow >= s, ai - ai[s:s + 1], -jnp.inf))
                sc = jnp.sum(qi * ki[s:s + 1] * e, axis=1, keepdims=True)
                acc = acc + sc * ii[s:s + 1]
            blocks.append(acc)
        o = o + jnp.concatenate(blocks, axis=0)
        st_ref[h] = st_t * jnp.exp(a_end[:, sl]) + _dot(ih.T.astype(BF16), kd_end[:, sl].astype(BF16))
        outs.append(_rms(o, nrm_ref[:, sl]))
    o_ref[0] = jnp.concatenate(outs, axis=-1) * jax.nn.silu(g_ref[0])


def hgrn2(proj, gamma, hg_norm, layer):
    b, s, _ = proj.shape
    w = D_HEADS * D_KEY
    col = lambda name: ODD_DST[name] // w
    spec = lambda name: pl.BlockSpec((1, HG_CHUNK, w), lambda i, j: (i, j, col(name)))
    return pl.pallas_call(
        functools.partial(_hgrn_kernel, layer=layer),
        grid=(b, s // HG_CHUNK),
        in_specs=[pl.BlockSpec(gamma.shape, lambda i, j: (0, 0)),
                  pl.BlockSpec((1, w), lambda i, j: (0, 0)),
                  spec("hq"), spec("hf"), spec("hi"), spec("hg")],
        out_specs=pl.BlockSpec((1, HG_CHUNK, w), lambda i, j: (i, j, 0)),
        out_shape=jax.ShapeDtypeStruct((b, s, w), F32),
        scratch_shapes=[pltpu.VMEM((D_HEADS, D_KEY, D_KEY), F32)],
        compiler_params=_params("parallel", "arbitrary"),
        name="hgrn2",
    )(gamma, hg_norm.reshape(1, w), proj, proj, proj, proj)


def _outproj_kernel(h_ref, a_ref, b_ref, w_ref, o_ref):
    ka = a_ref.shape[1]
    o_ref[...] = (h_ref[...] + _dot(a_ref[...].astype(BF16), w_ref[0:ka, :])
                  + _dot(b_ref[...].astype(BF16), w_ref[ka:, :]))


def outproj_residual(h, a, b, w, tm=512):
    m, d = h.shape
    return pl.pallas_call(
        _outproj_kernel,
        grid=(m // tm,),
        in_specs=[pl.BlockSpec((tm, d), lambda i: (i, 0)),
                  pl.BlockSpec((tm, a.shape[1]), lambda i: (i, 0)),
                  pl.BlockSpec((tm, b.shape[1]), lambda i: (i, 0)),
                  pl.BlockSpec(w.shape, lambda i: (0, 0))],
        out_specs=pl.BlockSpec((tm, d), lambda i: (i, 0)),
        out_shape=jax.ShapeDtypeStruct((m, d), F32),
        compiler_params=_params("parallel"),
        name="outproj_residual",
    )(h, a, b, w)


def _xattn_kernel(h_ref, g_ref, wq_ref, kv_ref, wo_ref, o_ref):
    x = h_ref[0]
    q = _dot(_rms(x, g_ref[...]).astype(BF16), wq_ref[...])
    kv = kv_ref[0]
    dm = X_HEADS * X_HEAD_DIM
    outs = []
    for h in range(X_HEADS):
        sl = slice(h * X_HEAD_DIM, (h + 1) * X_HEAD_DIM)
        logits = _dot_nt(q[:, sl].astype(BF16), kv[:, sl]) * (X_HEAD_DIM ** -0.5)
        p = jnp.exp(logits - jnp.max(logits, axis=1, keepdims=True))
        p = p / jnp.sum(p, axis=1, keepdims=True)
        outs.append(_dot(p.astype(BF16), kv[:, dm + h * X_HEAD_DIM:dm + (h + 1) * X_HEAD_DIM]).astype(BF16))
    o_ref[0] = x + _dot(jnp.concatenate(outs, axis=-1), wo_ref[...])


def memory_cross_attn(h, g, wq, kv, wo, ts=512):
    b, s, d = h.shape
    return pl.pallas_call(
        _xattn_kernel,
        grid=(b, s // ts),
        in_specs=[pl.BlockSpec((1, ts, d), lambda i, j: (i, j, 0)),
                  pl.BlockSpec((1, d), lambda i, j: (0, 0)),
                  pl.BlockSpec(wq.shape, lambda i, j: (0, 0)),
                  pl.BlockSpec((1,) + kv.shape[1:], lambda i, j: (i, 0, 0)),
                  pl.BlockSpec(wo.shape, lambda i, j: (0, 0))],
        out_specs=pl.BlockSpec((1, ts, d), lambda i, j: (i, j, 0)),
        out_shape=jax.ShapeDtypeStruct(h.shape, F32),
        compiler_params=_params("parallel", "parallel"),
        name="memory_cross_attn",
    )(h, g.reshape(1, d), wq, kv, wo)


def _extract_top(work, iota, rounds):
    rows = work.shape[0]
    rank = jnp.full(work.shape, NO_RANK, I32)
    vals = jnp.zeros((rounds, work.shape[1]), F32)
    riota = lax.broadcasted_iota(I32, (rounds, 1), 0)
    for r in range(rounds):
        m = jnp.max(work, axis=0, keepdims=True)
        first = jnp.min(jnp.where(work == m, iota, rows), axis=0, keepdims=True)
        hit = iota == first
        rank = jnp.where(hit, r, rank)
        work = jnp.where(hit, -jnp.inf, work)
        vals = jnp.where(riota == r, m, vals)
    return vals, rank


def _batcher_pairs(n):
    pairs = []
    p = 1
    while p < n:
        k = p
        while k >= 1:
            for j in range(k % p, n - k, 2 * k):
                for i in range(min(k, n - j - k)):
                    if (i + j) // (2 * p) == (i + j + k) // (2 * p):
                        pairs.append((i + j, i + j + k))
            k //= 2
        p *= 2
    return pairs


_SORT16 = _batcher_pairs(PEER_TOPK)


def _compare_exchange(w, i, j):
    a, b = w[i], w[j]
    if b is None:
        return
    if a is None:
        w[i], w[j] = b, None
        return
    w[i], w[j] = jnp.maximum(a, b), jnp.minimum(a, b)


def _sorted_top(slabs):
    k = PEER_TOPK
    w = list(slabs) + [None] * (k - len(slabs))
    for i, j in _SORT16:
        _compare_exchange(w, i, j)
    for shift in (4, 2, 1):
        r = [None if x is None else pltpu.roll(x, shift, axis=0) for x in w]
        t = []
        for i in range(k):
            a, b = w[i], r[k - 1 - i]
            t.append(b if a is None else a if b is None else jnp.maximum(a, b))
        w = t
        d = k // 2
        while d >= 1:
            for i in range(k):
                if i & d == 0:
                    _compare_exchange(w, i, i + d)
            d //= 2
    return w


def _slabs(x):
    return [x[8 * j:8 * j + 8] for j in range(x.shape[0] // 8)]


def _count_ge(slabs, thr):
    acc = None
    for x in slabs:
        c = jnp.where(x >= thr, 1.0, 0.0)
        acc = c if acc is None else acc + c
    return jnp.sum(acc, axis=0, keepdims=True)


def _route_head_fast(s1, s2, iota8):
    k = PEER_TOPK
    c1, c2 = _slabs(s1), _slabs(s2)
    a, b = _sorted_top(c1), _sorted_top(c2)
    tie = jnp.logical_or(_count_ge(c1, a[k - 1]) != k, _count_ge(c2, b[k - 1]) != k)
    for i in range(k - 1):
        tie = jnp.logical_or(tie, jnp.logical_or(a[i][0:1] <= a[i + 1][0:1], b[i][0:1] <= b[i + 1][0:1]))

    def stack8(vals):
        out = vals[0]
        for r in range(1, 8):
            out = jnp.where(iota8 == r, vals[r], out)
        return out

    a_hi, b_lo, b_hi = stack8(a[8:]), stack8(b[:8]), stack8(b[8:])
    cand = [a[0] + b_lo, a[0] + b_hi]
    for i in range(1, 8):
        cand.append(jnp.where(iota8 < k // (i + 1), a[i] + b_lo, -jnp.inf))
    cand.append(a_hi + b[0])
    thr = _sorted_top(cand)[k - 1]
    tie = jnp.logical_or(tie, _count_ge(cand, thr) != k)
    chosen = [jnp.where(x >= thr, 1.0, 0.0) for x in cand]
    top = a[0] + b[0]
    z = None
    for x, m in zip(cand, chosen):
        e = m * jnp.exp(x - top)
        z = e if z is None else z + e
    z = jnp.sum(z, axis=0, keepdims=True)
    n = [jnp.sum(chosen[0] + chosen[1], axis=0, keepdims=True)]
    n += [jnp.sum(chosen[i + 1], axis=0, keepdims=True) for i in range(1, 8)]
    n += [chosen[9][r:r + 1] for r in range(8)]
    n = [jnp.broadcast_to(x, (8, x.shape[1])) for x in n]
    n1, r2 = [], []
    for x1, x2 in zip(c1, c2):
        acc_n = jnp.zeros(x1.shape, F32)
        acc_r = jnp.full(x2.shape, float(NO_RANK), F32)
        for r in range(k - 1, -1, -1):
            acc_n = jnp.where(x1 >= a[r], n[r], acc_n)
            acc_r = jnp.where(x2 >= b[r], float(r), acc_r)
        n1.append(acc_n)
        r2.append(acc_r)
    a_tab = jnp.exp(s1 - a[0][0:1]) / z
    b_tab = jnp.exp(s2 - b[0][0:1])
    return a_tab, jnp.concatenate(n1, axis=0), b_tab, jnp.concatenate(r2, axis=0), tie


def _route_head_exact(s1, s2, iota_keys, iota8, iota_cand):
    k = PEER_TOPK
    tt = s1.shape[1]
    n_full = 8
    cand_rows = k + 8 * (n_full - 1) + 8
    a, rank1 = _extract_top(s1, iota_keys, k)
    b, rank2 = _extract_top(s2, iota_keys, k)
    blocks = [a[0:1] + b]
    for i in range(1, n_full):
        blocks.append(jnp.where(iota8 < k // (i + 1), a[i:i + 1] + b[0:8], -jnp.inf))
    blocks.append(a[8:16] + b[0:1])
    cand = jnp.concatenate(blocks, axis=0)
    _, crank = _extract_top(cand, iota_cand, k)
    chosen = crank < k
    z = jnp.sum(jnp.where(chosen, jnp.exp(cand - (a[0:1] + b[0:1])), 0.0), axis=0, keepdims=True)
    chosen_f = jnp.where(chosen, 1.0, 0.0)
    n_lo = jnp.zeros((8, tt), F32)
    n_lo = jnp.where(iota8 == 0, jnp.sum(chosen_f[0:k], axis=0, keepdims=True), n_lo)
    for i in range(1, n_full):
        r0 = k + 8 * (i - 1)
        n_lo = jnp.where(iota8 == i, jnp.sum(chosen_f[r0:r0 + 8], axis=0, keepdims=True), n_lo)
    n_hi = chosen_f[cand_rows - 8:cand_rows]
    n1 = jnp.zeros((N_KEYS, tt), F32)
    for r in range(k):
        n_r = n_lo[r:r + 1] if r < 8 else n_hi[r - 8:r - 7]
        n1 = jnp.where(rank1 == r, n_r, n1)
    return jnp.exp(s1 - a[0:1]) / z, n1, jnp.exp(s2 - b[0:1]), rank2.astype(F32)


def _route_kernel(h_ref, g_ref, wq_ref, keys_ref, hn_ref, a_ref, n1_ref, b_ref, r2_ref, q_s, *, tt):
    hn = _rms(h_ref[...], g_ref[...])
    hn_ref[...] = hn.T.astype(BF16)
    q_s[...] = _dot(hn.astype(BF16), wq_ref[...]).astype(BF16)
    iota8 = lax.broadcasted_iota(I32, (8, 1), 0)

    def scores(h):
        s1 = _dot_nt(keys_ref[h, 0], q_s[:, (2 * h) * N_KEYS:(2 * h + 1) * N_KEYS])
        s2 = _dot_nt(keys_ref[h, 1], q_s[:, (2 * h + 1) * N_KEYS:(2 * h + 2) * N_KEYS])
        return s1, s2

    def store(h, tabs):
        a_ref[h], n1_ref[h] = tabs[0], tabs[1]
        b_ref[h], r2_ref[h] = tabs[2].astype(BF16), tabs[3].astype(BF16)

    tie = jnp.zeros((1, tt), jnp.bool_)
    for h in range(PEER_HEADS):
        tabs = _route_head_fast(*scores(h), iota8)
        store(h, tabs[:4])
        tie = jnp.logical_or(tie, tabs[4])

    @pl.when(jnp.max(jnp.where(tie, 1.0, 0.0)) > 0.0)
    def _():
        iota_keys = lax.broadcasted_iota(I32, (N_KEYS, tt), 0)
        iota_cand = lax.broadcasted_iota(I32, (PEER_TOPK + 8 * 8, tt), 0)
        for h in range(PEER_HEADS):
            store(h, _route_head_exact(*scores(h), iota_keys, iota8, iota_cand))


def peer_route(h2d, g, wq, keys, tt=128):
    t, d = h2d.shape
    tab = jax.ShapeDtypeStruct((PEER_HEADS, N_KEYS, t), F32)
    tab_bf = jax.ShapeDtypeStruct((PEER_HEADS, N_KEYS, t), BF16)
    tab_spec = pl.BlockSpec((PEER_HEADS, N_KEYS, tt), lambda i: (0, 0, i))
    return pl.pallas_call(
        functools.partial(_route_kernel, tt=tt),
        grid=(t // tt,),
        in_specs=[pl.BlockSpec((tt, d), lambda i: (i, 0)),
                  pl.BlockSpec((1, d), lambda i: (0, 0)),
                  pl.BlockSpec(wq.shape, lambda i: (0, 0)),
                  pl.BlockSpec(keys.shape, lambda i: (0, 0, 0, 0))],
        out_specs=[pl.BlockSpec((d, tt), lambda i: (0, i)), tab_spec, tab_spec, tab_spec, tab_spec],
        out_shape=[jax.ShapeDtypeStruct((d, t), BF16), tab, tab, tab_bf, tab_bf],
        scratch_shapes=[pltpu.VMEM((tt, wq.shape[1]), BF16)],
        compiler_params=_params("parallel"),
        name="peer_route",
    )(h2d, g.reshape(1, d), wq, keys)


def _peer_dense_kernel(h_ref, hn_ref, a_ref, n1_ref, b_ref, r2_ref, u_ref, vt_ref, fg_ref, o_ref, acc_ref, p0_ref, p1_ref,
                       *, te, final_norm):
    e = pl.program_id(1)
    n_tiles = pl.num_programs(1) - 1

    @pl.when(e == 0)
    def _():
        acc_ref[...] = jnp.zeros(acc_ref.shape, F32)
        p1_ref[...] = jnp.zeros(p1_ref.shape, BF16)

    hn = hn_ref[...]
    tt = hn.shape[1]
    keys_per_sub = PEER_SUB // N_KEYS
    tile = jnp.minimum(e, n_tiles - 1)

    def step(p_cur, p_prev):
        n_sub = te // PEER_SUB
        d_rows = acc_ref.shape[0] // n_sub

        def activations(sb):
            r0 = sb * PEER_SUB
            p_cur[r0:r0 + PEER_SUB, :] = jax.nn.gelu(_dot(u_ref[r0:r0 + PEER_SUB, :], hn).astype(BF16))

        activations(0)
        for sb in range(n_sub):
            if sb + 1 < n_sub:
                activations(sb + 1)
            out_rows = slice(sb * d_rows, (sb + 1) * d_rows)
            acc_ref[out_rows, :] += _dot(vt_ref[out_rows, :], p_prev[...])
            r0 = sb * PEER_SUB
            for kb in range(keys_per_sub):
                i1 = tile * (te // N_KEYS) + sb * keys_per_sub + kb
                rows = slice(r0 + kb * N_KEYS, r0 + (kb + 1) * N_KEYS)
                a_full = [a_ref[h, pl.ds(i1, 1), :].astype(BF16) for h in range(PEER_HEADS)]
                n_full = [n1_ref[h, pl.ds(i1, 1), :].astype(BF16) for h in range(PEER_HEADS)]
                for c in range(tt // PEER_COLS):
                    cols = slice(c * PEER_COLS, (c + 1) * PEER_COLS)
                    gate = jnp.zeros((N_KEYS, PEER_COLS), BF16)
                    for h in range(PEER_HEADS):
                        gate = gate + jnp.where(r2_ref[h, :, cols] < n_full[h][:, cols],
                                                b_ref[h, :, cols] * a_full[h][:, cols], jnp.zeros((), BF16))
                    p_cur[rows, cols] = gate * p_cur[rows, cols]

    @pl.when(e % 2 == 0)
    def _():
        step(p0_ref, p1_ref)

    @pl.when(e % 2 == 1)
    def _():
        step(p1_ref, p0_ref)

    @pl.when(e == n_tiles)
    def _():
        out = h_ref[...] + acc_ref[...].T
        if final_norm:
            out = _rms(out, fg_ref[...])
        o_ref[...] = out


def peer_dense(h2d, hn_t, a, n1, b, r2, u, vt, final_g, final_norm, tt=512, te=1024):
    t, d = h2d.shape
    n_tiles = u.shape[0] // te
    tab_spec = pl.BlockSpec((PEER_HEADS, N_KEYS, tt), lambda i, j: (0, 0, i))
    return pl.pallas_call(
        functools.partial(_peer_dense_kernel, te=te, final_norm=final_norm),
        grid=(t // tt, n_tiles + 1),
        in_specs=[pl.BlockSpec((tt, d), lambda i, j: (i, 0)),
                  pl.BlockSpec((d, tt), lambda i, j: (0, i)),
                  tab_spec, tab_spec, tab_spec, tab_spec,
                  pl.BlockSpec((te, d), lambda i, j: (jnp.minimum(j, n_tiles - 1), 0)),
                  pl.BlockSpec((d, te), lambda i, j: (0, jnp.maximum(j - 1, 0))),
                  pl.BlockSpec((1, d), lambda i, j: (0, 0))],
        out_specs=pl.BlockSpec((tt, d), lambda i, j: (i, 0)),
        out_shape=jax.ShapeDtypeStruct((t, d), F32),
        scratch_shapes=[pltpu.VMEM((d, tt), F32), pltpu.VMEM((te, tt), BF16), pltpu.VMEM((te, tt), BF16)],
        compiler_params=_params("parallel", "arbitrary"),
        name="peer_dense",
    )(h2d, hn_t, a, n1, b, r2, u, vt, final_g.reshape(1, d))


def _pad_odd_w_in(w):
    out = jnp.zeros((w.shape[0], ODD_COLS), w.dtype)
    for name, (src, width) in ODD_SRC.items():
        out = lax.dynamic_update_slice(out, w[:, src:src + width], (0, ODD_DST[name]))
    return out


def _block_diag_uv(w_uv):
    out = jnp.zeros((C_HEADS * C_LATENT, C_HEADS * C_HEAD_DIM), w_uv.dtype)
    for h in range(C_HEADS):
        out = lax.dynamic_update_slice(out, w_uv[h], (h * C_LATENT, h * C_HEAD_DIM))
    return out


def kernel(x, mem, mix_norm, even_w_in, even_conv_w, even_pool_w, even_pool_scale, even_w_out, odd_w_in, odd_kv_norm, odd_w_uv, odd_hg_norm, odd_w_out, hgrn_gamma, rel_bias, mem_norm, xattn_norm, xattn_wq, xattn_wkv, xattn_wo, peer_norm, peer_wq, peer_keys, peer_u, peer_v, final_norm):
    bsz, s, d = x.shape
    t = bsz * s
    depth = mix_norm.shape[0]
    mem2d = mem.reshape(-1, d)
    h = x
    for l in range(depth):
        j = l // 2
        if l % 2 == 0:
            h = even_mixer(h, mix_norm[l], even_w_in[j].astype(BF16), even_conv_w[j].reshape(3, A_WIDTH),
                           even_pool_w[j].astype(BF16), even_pool_scale[j].reshape(1, -1),
                           even_w_out[j].astype(BF16))
        else:
            proj = norm_matmul(h.reshape(t, d), mix_norm[l], _pad_odd_w_in(odd_w_in[j]).astype(BF16), F32)
            proj = proj.reshape(bsz, s, ODD_COLS)
            y_c = dsa_attention(proj, rel_bias, odd_kv_norm[j], _block_diag_uv(odd_w_uv[j]).astype(BF16))
            y_d = hgrn2(proj, hgrn_gamma, odd_hg_norm[j], l)
            h = outproj_residual(h.reshape(t, d), y_c.reshape(t, -1), y_d.reshape(t, -1),
                                 odd_w_out[j].astype(BF16)).reshape(bsz, s, d)
        kv = norm_matmul(mem2d, mem_norm, xattn_wkv[l].astype(BF16), BF16).reshape(bsz, -1, 2 * d)
        h = memory_cross_attn(h, xattn_norm[l], xattn_wq[l].astype(BF16), kv, xattn_wo[l].astype(BF16))
        h2d = h.reshape(t, d)
        hn, a, n1, b, r2 = peer_route(h2d, peer_norm[l], peer_wq[l].astype(BF16), peer_keys[l].astype(BF16))
        h = peer_dense(h2d, hn, a, n1, b, r2, peer_u[l].astype(BF16), peer_v[l].T.astype(BF16),
                       final_norm, l == depth - 1).reshape(bsz, s, d)
    return h
```

```python
import functools
import math

import numpy as np
import jax
import jax.numpy as jnp
from jax import lax
from jax.experimental import pallas as pl
from jax.experimental.pallas import tpu as pltpu

F32 = jnp.float32
BF16 = jnp.bfloat16
I32 = jnp.int32

EPS = 1e-6
LANES = 128
INT_MIN = -(2 ** 31)
VMEM_LIMIT = 56 * 1024 * 1024

POOL_WINDOWS = (2, 4, 8, 16)
HALO = 16
A_WIDTH = 512
POOL_GROUP = 128

C_HEADS = 8
C_LATENT = 128
C_HEAD_DIM = 64
IDX_HEADS = 8
IDX_DIM = 64
DSA_TOPK_MAX = 256
Q_BLOCK = 128
DSA_TIERS = 4
REL_BUCKETS = 32
REL_MAX_DIST = 128

D_HEADS = 4
D_KEY = 128
HG_CHUNK = 128
HG_SUB = 16

X_HEADS = 4
X_HEAD_DIM = 256

N_KEYS = 128
PEER_HEADS = 8
PEER_TOPK = 16
NO_RANK = 127
PEER_SUB = 256
PEER_COLS = 256

ODD_COLS = 4096
ODD_SRC = dict(q_lat=(0, 1024), c_kv=(1024, 128), iq=(1152, 512), ik=(1664, 64), iw=(1728, 8),
               hq=(1736, 512), hf=(2248, 512), hi=(2760, 512), hg=(3272, 512))
ODD_DST = dict(q_lat=0, iq=1024, c_kv=1536, ik=1664, iw=1792, hq=2048, hf=2560, hi=3072, hg=3584)

NT_DIMS = (((1,), (1,)), ((), ()))


def _params(*sem):
    return pltpu.CompilerParams(dimension_semantics=sem, vmem_limit_bytes=VMEM_LIMIT)


def _rms(x, g):
    return x * lax.rsqrt(jnp.mean(x * x, axis=-1, keepdims=True) + EPS) * g


def _dot(a, b):
    return jnp.dot(a, b, preferred_element_type=F32)


def _dot_nt(a, b):
    return lax.dot_general(a, b, NT_DIMS, preferred_element_type=F32)


def _norm_matmul_kernel(x_ref, g_ref, w_ref, o_ref):
    y = _rms(x_ref[...], g_ref[...])
    o_ref[...] = _dot(y.astype(BF16), w_ref[...]).astype(o_ref.dtype)


def norm_matmul(x, g, w, out_dtype, tm=256):
    m, d = x.shape
    n = w.shape[1]
    return pl.pallas_call(
        _norm_matmul_kernel,
        grid=(m // tm,),
        in_specs=[pl.BlockSpec((tm, d), lambda i: (i, 0)),
                  pl.BlockSpec((1, d), lambda i: (0, 0)),
                  pl.BlockSpec((d, n), lambda i: (0, 0))],
        out_specs=pl.BlockSpec((tm, n), lambda i: (i, 0)),
        out_shape=jax.ShapeDtypeStruct((m, n), out_dtype),
        compiler_params=_params("parallel"),
        name="norm_matmul",
    )(x, g.reshape(1, d), w)


def _even_kernel(x_ref, g_ref, win_ref, cw_ref, pw_ref, ps_ref, wout_ref, o_ref, vbuf, pbuf, *, ts):
    j = pl.program_id(1)

    @pl.when(j == 0)
    def _():
        vbuf[0:HALO, :] = jnp.zeros((HALO, A_WIDTH), F32)
        pbuf[0:HALO, :] = jnp.zeros((HALO, A_WIDTH), F32)

    x = x_ref[0]
    proj = _dot(_rms(x, g_ref[...]).astype(BF16), win_ref[...])
    u = proj[:, 0:512]
    gate_c = proj[:, 512:1024]
    gate_b = proj[:, 1024:1536]
    pv = proj[:, 1536:2048]
    vbuf[HALO:HALO + ts, :] = gate_c * u
    pbuf[HALO:HALO + ts, :] = pv

    cw = cw_ref[...]
    z = (cw[2:3] * vbuf[HALO:HALO + ts, :] + cw[1:2] * vbuf[HALO - 1:HALO - 1 + ts, :]
         + cw[0:1] * vbuf[HALO - 2:HALO - 2 + ts, :])
    y_a = gate_b * z

    pos = j * ts + lax.broadcasted_iota(I32, (ts, 1), 0)
    y_b = []
    for gi, w in enumerate(POOL_WINDOWS):
        c0 = gi * POOL_GROUP
        acc = pbuf[HALO:HALO + ts, c0:c0 + POOL_GROUP]
        for k in range(1, w):
            acc = acc + pbuf[HALO - k:HALO - k + ts, c0:c0 + POOL_GROUP]
        cnt = jnp.minimum(pos + 1, w).astype(F32)
        pooled = acc / cnt - pv[:, c0:c0 + POOL_GROUP]
        y_b.append(_dot(pooled.astype(BF16), pw_ref[gi]))
    y_b = jnp.concatenate(y_b, axis=-1) * ps_ref[...]

    o_ref[0] = (x + _dot(y_a.astype(BF16), wout_ref[0:A_WIDTH, :])
                + _dot(y_b.astype(BF16), wout_ref[A_WIDTH:2 * A_WIDTH, :]))

    vbuf[0:HALO, :] = vbuf[ts:ts + HALO, :]
    pbuf[0:HALO, :] = pbuf[ts:ts + HALO, :]


def even_mixer(h, g, w_in, conv_w, pool_w, pool_scale, w_out, ts=512):
    b, s, d = h.shape
    return pl.pallas_call(
        functools.partial(_even_kernel, ts=ts),
        grid=(b, s // ts),
        in_specs=[pl.BlockSpec((1, ts, d), lambda i, j: (i, j, 0)),
                  pl.BlockSpec((1, d), lambda i, j: (0, 0)),
                  pl.BlockSpec(w_in.shape, lambda i, j: (0, 0)),
                  pl.BlockSpec(conv_w.shape, lambda i, j: (0, 0)),
                  pl.BlockSpec(pool_w.shape, lambda i, j: (0, 0, 0)),
                  pl.BlockSpec(pool_scale.shape, lambda i, j: (0, 0)),
                  pl.BlockSpec(w_out.shape, lambda i, j: (0, 0))],
        out_specs=pl.BlockSpec((1, ts, d), lambda i, j: (i, j, 0)),
        out_shape=jax.ShapeDtypeStruct(h.shape, F32),
        scratch_shapes=[pltpu.VMEM((HALO + ts, A_WIDTH), F32), pltpu.VMEM((HALO + ts, A_WIDTH), F32)],
        compiler_params=_params("parallel", "arbitrary"),
        name="even_mixer",
    )(h, g.reshape(1, d), w_in, conv_w, pool_w, pool_scale, w_out)


def _t5_bucket_table():
    t = np.arange(Q_BLOCK)[:, None]
    s = np.arange(Q_BLOCK)[None, :]
    dist = np.concatenate([t - s, t - s + Q_BLOCK], axis=1)
    max_exact = REL_BUCKETS // 2
    d = np.maximum(dist, 0)
    log_ratio = (np.log(np.maximum(d, 1).astype(np.float32) / np.float32(max_exact))
                 / np.float32(math.log(REL_MAX_DIST / max_exact)))
    large = max_exact + (log_ratio * np.float32(REL_BUCKETS - max_exact)).astype(np.int32)
    large = np.minimum(large, REL_BUCKETS - 1)
    return np.where(d < max_exact, d, large).astype(np.int32)


def _dsa_kernel(rb_ref, bk_ref, q_ref, iq_ref, iw_ref, ckv_ref, ik_ref, kvn_ref, wuv_ref, o_ref,
                ckv_s, ik_s, bias_s, key_s, sel_s, lg_s, *, s_len, topk):
    qi = pl.program_id(1)

    @pl.when(qi == 0)
    def _():
        ckv_s[...] = _rms(ckv_ref[0], kvn_ref[...]).astype(BF16)
        ik_s[...] = ik_ref[0][:, 0:IDX_DIM].astype(BF16)
        bk = bk_ref[...]
        for h in range(C_HEADS):
            tile = jnp.zeros(bk.shape, F32)
            for k in range(REL_BUCKETS):
                tile = jnp.where(bk == k, rb_ref[k, h], tile)
            bias_s[h] = tile

    n_q = s_len // Q_BLOCK
    tiers = DSA_TIERS if n_q % DSA_TIERS == 0 else 1
    for tier in range(tiers):
        lo, hi = tier * n_q // tiers, (tier + 1) * n_q // tiers

        @pl.when(jnp.logical_and(qi >= lo, qi < hi))
        def _(s_eff=hi * Q_BLOCK):
            _dsa_block(rb_ref, q_ref, iq_ref, iw_ref, wuv_ref, o_ref, ckv_s, ik_s, bias_s, key_s, sel_s, lg_s,
                       qi=qi, s_eff=s_eff, topk=topk)


def _dsa_block(rb_ref, q_ref, iq_ref, iw_ref, wuv_ref, o_ref, ckv_s, ik_s, bias_s, key_s, sel_s, lg_s, *, qi, s_eff, topk):
    scale = C_LATENT ** -0.5
    far_bucket = REL_BUCKETS - 1
    t_idx = qi * Q_BLOCK + lax.broadcasted_iota(I32, (Q_BLOCK, 1), 0)
    s_idx = lax.broadcasted_iota(I32, (1, s_eff), 1)
    causal = s_idx <= t_idx
    keys = slice(0, s_eff)

    iq = iq_ref[0]
    iw = iw_ref[0] * (IDX_HEADS ** -0.5 * IDX_DIM ** -0.5)
    ik = ik_s[keys, :]
    score = jnp.zeros((Q_BLOCK, s_eff), F32)
    for h in range(IDX_HEADS):
        s_h = _dot_nt(iq[:, h * IDX_DIM:(h + 1) * IDX_DIM].astype(BF16), ik)
        score = score + jnp.maximum(s_h, 0.0) * iw[:, h:h + 1]

    bits = lax.bitcast_convert_type(score + 0.0, I32)
    key = jnp.where(bits < 0, bits ^ jnp.int32(0x7FFFFFFF), bits)
    key_s[:, keys] = jnp.where(causal, key, jnp.int32(INT_MIN))

    def count(mask):
        return jnp.sum(jnp.where(mask, 1.0, 0.0), axis=1, keepdims=True)

    def thr_step(i, ans):
        cand = ans + lax.shift_left(jnp.int32(1), 31 - i)
        return jnp.where(count(key_s[:, keys] >= cand) >= topk, cand, ans)

    thr = lax.fori_loop(0, 32, thr_step, jnp.full((Q_BLOCK, 1), INT_MIN, I32))
    key = key_s[:, keys]
    gt = key > thr
    eq = key == thr
    need = topk - count(gt)
    tie_rows = jnp.logical_and(count(eq) > need, thr != jnp.int32(INT_MIN))
    any_tie = jnp.max(jnp.where(tie_rows, 1.0, 0.0)) > 0.0
    cut_bits = s_eff.bit_length()

    def tie_cut():
        def step(i, cut):
            cand = cut + lax.shift_left(jnp.int32(1), cut_bits - 1 - i)
            c = count(jnp.logical_and(key_s[:, keys] == thr, s_idx < cand))
            return jnp.where(c <= need, cand, cut)
        return lax.fori_loop(0, cut_bits, step, jnp.zeros((Q_BLOCK, 1), I32))

    cut = lax.cond(any_tie, tie_cut, lambda: jnp.full((Q_BLOCK, 1), s_eff, I32))
    sel = jnp.logical_and(causal, jnp.logical_or(gt, jnp.logical_and(eq, s_idx < cut)))
    sel_s[:, keys] = jnp.where(sel, 0.0, -jnp.inf)

    off = pl.multiple_of(qi * Q_BLOCK, Q_BLOCK)
    off_prev = pl.multiple_of(jnp.maximum(qi - 1, 0) * Q_BLOCK, Q_BLOCK)
    ckv = ckv_s[keys, :]
    lats = []
    for h in range(C_HEADS):
        qh = (q_ref[0][:, h * C_LATENT:(h + 1) * C_LATENT] * scale).astype(BF16)
        far = rb_ref[far_bucket, h]
        lg_s[:, keys] = _dot_nt(qh, ckv) + sel_s[:, keys]
        lg_s[:, pl.ds(off, Q_BLOCK)] = (_dot_nt(qh, ckv_s[pl.ds(off, Q_BLOCK), :])
                                        + (bias_s[h][:, 0:Q_BLOCK] - far) + sel_s[:, pl.ds(off, Q_BLOCK)])

        @pl.when(qi > 0)
        def _():
            lg_s[:, pl.ds(off_prev, Q_BLOCK)] = (_dot_nt(qh, ckv_s[pl.ds(off_prev, Q_BLOCK), :])
                                                 + (bias_s[h][:, Q_BLOCK:2 * Q_BLOCK] - far)
                                                 + sel_s[:, pl.ds(off_prev, Q_BLOCK)])

        logits = lg_s[:, keys]
        p = jnp.exp(logits - jnp.max(logits, axis=1, keepdims=True))
        lat = _dot(p.astype(BF16), ckv) / jnp.sum(p, axis=1, keepdims=True)
        lats.append(lat.astype(BF16))
    o_ref[0] = _dot(jnp.concatenate(lats, axis=-1), wuv_ref[...])


def dsa_attention(proj, rel_bias, kv_norm, wuv_bd):
    b, s, _ = proj.shape
    topk = min(DSA_TOPK_MAX, s // 4)
    bk = jnp.asarray(_t5_bucket_table())
    col = lambda name, width: ODD_DST[name] // width
    return pl.pallas_call(
        functools.partial(_dsa_kernel, s_len=s, topk=topk),
        grid=(b, s // Q_BLOCK),
        in_specs=[pl.BlockSpec(memory_space=pltpu.SMEM),
                  pl.BlockSpec(bk.shape, lambda i, j: (0, 0)),
                  pl.BlockSpec((1, Q_BLOCK, 1024), lambda i, j: (i, j, col("q_lat", 1024))),
                  pl.BlockSpec((1, Q_BLOCK, 512), lambda i, j: (i, j, col("iq", 512))),
                  pl.BlockSpec((1, Q_BLOCK, LANES), lambda i, j: (i, j, col("iw", LANES))),
                  pl.BlockSpec((1, s, LANES), lambda i, j: (i, 0, col("c_kv", LANES))),
                  pl.BlockSpec((1, s, LANES), lambda i, j: (i, 0, col("ik", LANES))),
                  pl.BlockSpec((1, C_LATENT), lambda i, j: (0, 0)),
                  pl.BlockSpec(wuv_bd.shape, lambda i, j: (0, 0))],
        out_specs=pl.BlockSpec((1, Q_BLOCK, C_HEADS * C_HEAD_DIM), lambda i, j: (i, j, 0)),
        out_shape=jax.ShapeDtypeStruct((b, s, C_HEADS * C_HEAD_DIM), F32),
        scratch_shapes=[pltpu.VMEM((s, C_LATENT), BF16),
                        pltpu.VMEM((s, IDX_DIM), BF16),
                        pltpu.VMEM((C_HEADS, Q_BLOCK, 2 * Q_BLOCK), F32),
                        pltpu.VMEM((Q_BLOCK, s), I32),
                        pltpu.VMEM((Q_BLOCK, s), F32),
                        pltpu.VMEM((Q_BLOCK, s), F32)],
        compiler_params=_params("parallel", "arbitrary"),
        name="dsa_attention",
    )(rel_bias, bk, proj, proj, proj, proj, proj, kv_norm.reshape(1, C_LATENT), wuv_bd)


def _hgrn_kernel(gam_ref, nrm_ref, q_ref, f_ref, i_ref, g_ref, o_ref, st_ref, *, layer):
    c = pl.program_id(1)
    ch = HG_CHUNK

    @pl.when(c == 0)
    def _():
        st_ref[...] = jnp.zeros(st_ref.shape, F32)

    gam = gam_ref[...]
    pe = jnp.exp(gam - jnp.max(gam, axis=0, keepdims=True))
    pe = pe / jnp.sum(pe, axis=0, keepdims=True)
    lb = jnp.zeros((1, gam.shape[1]), F32)
    for i in range(layer):
        lb = lb + pe[i:i + 1]

    f = lb + (1.0 - lb) * jax.nn.sigmoid(f_ref[0])
    kk = 1.0 - f
    a = jnp.log(f)
    row = lax.broadcasted_iota(I32, (ch, 1), 0)
    sh = 1
    while sh < ch:
        a = a + jnp.where(row >= sh, pltpu.roll(a, sh, axis=0), 0.0)
        sh *= 2
    q = q_ref[0]
    inp = i_ref[0]
    a_end = a[ch - 1:ch, :]
    qe = q * jnp.exp(a)
    kd_end = kk * jnp.exp(a_end - a)
    trow = lax.broadcasted_iota(I32, (HG_SUB, 1), 0)

    outs = []
    for h in range(D_HEADS):
        sl = slice(h * D_KEY, (h + 1) * D_KEY)
        ah, qh, kh, ih = a[:, sl], q[:, sl], kk[:, sl], inp[:, sl]
        ih_bf = ih.astype(BF16)
        st_t = st_ref[h]
        o = _dot_nt(qe[:, sl].astype(BF16), st_t.astype(BF16))
        blocks = []
        for i in range(ch // HG_SUB):
            r0 = i * HG_SUB
            ai, qi, ki, ii = ah[r0:r0 + HG_SUB], qh[r0:r0 + HG_SUB], kh[r0:r0 + HG_SUB], ih[r0:r0 + HG_SUB]
            acc = jnp.zeros((HG_SUB, D_KEY), F32)
            if i > 0:
                r_i = ah[r0 - 1:r0]
                qd = qi * jnp.exp(ai - r_i)
                kd = kh[0:r0] * jnp.exp(r_i - ah[0:r0])
                s_off = _dot_nt(qd.astype(BF16), kd.astype(BF16))
                acc = _dot(s_off.astype(BF16), ih_bf[0:r0])
            for s in range(HG_SUB):
                e = jnp.exp(jnp.where(trow >= s, ai - ai[s:s + 1], -jnp.inf))
                sc = jnp.sum(qi * ki[s:s + 1] * e, axis=1, keepdims=True)
                acc = acc + sc * ii[s:s + 1]
            blocks.append(acc)
        o = o + jnp.concatenate(blocks, axis=0)
        st_ref[h] = st_t * jnp.exp(a_end[:, sl]) + _dot(ih.T.astype(BF16), kd_end[:, sl].astype(BF16))
        outs.append(_rms(o, nrm_ref[:, sl]))
    o_ref[0] = jnp.concatenate(outs, axis=-1) * jax.nn.silu(g_ref[0])


def hgrn2(proj, gamma, hg_norm, layer):
    b, s, _ = proj.shape
    w = D_HEADS * D_KEY
    col = lambda name: ODD_DST[name] // w
    spec = lambda name: pl.BlockSpec((1, HG_CHUNK, w), lambda i, j: (i, j, col(name)))
    return pl.pallas_call(
        functools.partial(_hgrn_kernel, layer=layer),
        grid=(b, s // HG_CHUNK),
        in_specs=[pl.BlockSpec(gamma.shape, lambda i, j: (0, 0)),
                  pl.BlockSpec((1, w), lambda i, j: (0, 0)),
                  spec("hq"), spec("hf"), spec("hi"), spec("hg")],
        out_specs=pl.BlockSpec((1, HG_CHUNK, w), lambda i, j: (i, j, 0)),
        out_shape=jax.ShapeDtypeStruct((b, s, w), F32),
        scratch_shapes=[pltpu.VMEM((D_HEADS, D_KEY, D_KEY), F32)],
        compiler_params=_params("parallel", "arbitrary"),
        name="hgrn2",
    )(gamma, hg_norm.reshape(1, w), proj, proj, proj, proj)


def _outproj_kernel(h_ref, a_ref, b_ref, w_ref, o_ref):
    ka = a_ref.shape[1]
    o_ref[...] = (h_ref[...] + _dot(a_ref[...].astype(BF16), w_ref[0:ka, :])
                  + _dot(b_ref[...].astype(BF16), w_ref[ka:, :]))


def outproj_residual(h, a, b, w, tm=512):
    m, d = h.shape
    return pl.pallas_call(
        _outproj_kernel,
        grid=(m // tm,),
        in_specs=[pl.BlockSpec((tm, d), lambda i: (i, 0)),
                  pl.BlockSpec((tm, a.shape[1]), lambda i: (i, 0)),
                  pl.BlockSpec((tm, b.shape[1]), lambda i: (i, 0)),
                  pl.BlockSpec(w.shape, lambda i: (0, 0))],
        out_specs=pl.BlockSpec((tm, d), lambda i: (i, 0)),
        out_shape=jax.ShapeDtypeStruct((m, d), F32),
        compiler_params=_params("parallel"),
        name="outproj_residual",
    )(h, a, b, w)


def _xattn_kernel(h_ref, g_ref, wq_ref, kv_ref, wo_ref, o_ref):
    x = h_ref[0]
    q = _dot(_rms(x, g_ref[...]).astype(BF16), wq_ref[...])
    kv = kv_ref[0]
    dm = X_HEADS * X_HEAD_DIM
    outs = []
    for h in range(X_HEADS):
        sl = slice(h * X_HEAD_DIM, (h + 1) * X_HEAD_DIM)
        logits = _dot_nt(q[:, sl].astype(BF16), kv[:, sl]) * (X_HEAD_DIM ** -0.5)
        p = jnp.exp(logits - jnp.max(logits, axis=1, keepdims=True))
        p = p / jnp.sum(p, axis=1, keepdims=True)
        outs.append(_dot(p.astype(BF16), kv[:, dm + h * X_HEAD_DIM:dm + (h + 1) * X_HEAD_DIM]).astype(BF16))
    o_ref[0] = x + _dot(jnp.concatenate(outs, axis=-1), wo_ref[...])


def memory_cross_attn(h, g, wq, kv, wo, ts=512):
    b, s, d = h.shape
    return pl.pallas_call(
        _xattn_kernel,
        grid=(b, s // ts),
        in_specs=[pl.BlockSpec((1, ts, d), lambda i, j: (i, j, 0)),
                  pl.BlockSpec((1, d), lambda i, j: (0, 0)),
                  pl.BlockSpec(wq.shape, lambda i, j: (0, 0)),
                  pl.BlockSpec((1,) + kv.shape[1:], lambda i, j: (i, 0, 0)),
                  pl.BlockSpec(wo.shape, lambda i, j: (0, 0))],
        out_specs=pl.BlockSpec((1, ts, d), lambda i, j: (i, j, 0)),
        out_shape=jax.ShapeDtypeStruct(h.shape, F32),
        compiler_params=_params("parallel", "parallel"),
        name="memory_cross_attn",
    )(h, g.reshape(1, d), wq, kv, wo)


def _extract_top(work, iota, rounds):
    rows = work.shape[0]
    rank = jnp.full(work.shape, NO_RANK, I32)
    vals = jnp.zeros((rounds, work.shape[1]), F32)
    riota = lax.broadcasted_iota(I32, (rounds, 1), 0)
    for r in range(rounds):
        m = jnp.max(work, axis=0, keepdims=True)
        first = jnp.min(jnp.where(work == m, iota, rows), axis=0, keepdims=True)
        hit = iota == first
        rank = jnp.where(hit, r, rank)
        work = jnp.where(hit, -jnp.inf, work)
        vals = jnp.where(riota == r, m, vals)
    return vals, rank


def _batcher_pairs(n):
    pairs = []
    p = 1
    while p < n:
        k = p
        while k >= 1:
            for j in range(k % p, n - k, 2 * k):
                for i in range(min(k, n - j - k)):
                    if (i + j) // (2 * p) == (i + j + k) // (2 * p):
                        pairs.append((i + j, i + j + k))
            k //= 2
        p *= 2
    return pairs


_SORT16 = _batcher_pairs(PEER_TOPK)


def _compare_exchange(w, i, j):
    a, b = w[i], w[j]
    if b is None:
        return
    if a is None:
        w[i], w[j] = b, None
        return
    w[i], w[j] = jnp.maximum(a, b), jnp.minimum(a, b)


def _sorted_top(slabs):
    k = PEER_TOPK
    w = list(slabs) + [None] * (k - len(slabs))
    for i, j in _SORT16:
        _compare_exchange(w, i, j)
    for shift in (4, 2, 1):
        r = [None if x is None else pltpu.roll(x, shift, axis=0) for x in w]
        t = []
        for i in range(k):
            a, b = w[i], r[k - 1 - i]
            t.append(b if a is None else a if b is None else jnp.maximum(a, b))
        w = t
        d = k // 2
        while d >= 1:
            for i in range(k):
                if i & d == 0:
                    _compare_exchange(w, i, i + d)
            d //= 2
    return w


def _slabs(x):
    return [x[8 * j:8 * j + 8] for j in range(x.shape[0] // 8)]


def _count_ge(slabs, thr):
    acc = None
    for x in slabs:
        c = jnp.where(x >= thr, 1.0, 0.0)
        acc = c if acc is None else acc + c
    return jnp.sum(acc, axis=0, keepdims=True)


def _route_head_fast(s1, s2, iota8):
    k = PEER_TOPK
    c1, c2 = _slabs(s1), _slabs(s2)
    a, b = _sorted_top(c1), _sorted_top(c2)
    tie = jnp.logical_or(_count_ge(c1, a[k - 1]) != k, _count_ge(c2, b[k - 1]) != k)
    for i in range(k - 1):
        tie = jnp.logical_or(tie, jnp.logical_or(a[i][0:1] <= a[i + 1][0:1], b[i][0:1] <= b[i + 1][0:1]))

    def stack8(vals):
        out = vals[0]
        for r in range(1, 8):
            out = jnp.where(iota8 == r, vals[r], out)
        return out

    a_hi, b_lo, b_hi = stack8(a[8:]), stack8(b[:8]), stack8(b[8:])
    cand = [a[0] + b_lo, a[0] + b_hi]
    for i in range(1, 8):
        cand.append(jnp.where(iota8 < k // (i + 1), a[i] + b_lo, -jnp.inf))
    cand.append(a_hi + b[0])
    thr = _sorted_top(cand)[k - 1]
    tie = jnp.logical_or(tie, _count_ge(cand, thr) != k)
    chosen = [jnp.where(x >= thr, 1.0, 0.0) for x in cand]
    top = a[0] + b[0]
    z = None
    for x, m in zip(cand, chosen):
        e = m * jnp.exp(x - top)
        z = e if z is None else z + e
    z = jnp.sum(z, axis=0, keepdims=True)
    n = [jnp.sum(chosen[0] + chosen[1], axis=0, keepdims=True)]
    n += [jnp.sum(chosen[i + 1], axis=0, keepdims=True) for i in range(1, 8)]
    n += [chosen[9][r:r + 1] for r in range(8)]
    n = [jnp.broadcast_to(x, (8, x.shape[1])) for x in n]
    n1, r2 = [], []
    for x1, x2 in zip(c1, c2):
        acc_n = jnp.zeros(x1.shape, F32)
        acc_r = jnp.full(x2.shape, float(NO_RANK), F32)
        for r in range(k - 1, -1, -1):
            acc_n = jnp.where(x1 >= a[r], n[r], acc_n)
            acc_r = jnp.where(x2 >= b[r], float(r), acc_r)
        n1.append(acc_n)
        r2.append(acc_r)
    a_tab = jnp.exp(s1 - a[0][0:1]) / z
    b_tab = jnp.exp(s2 - b[0][0:1])
    return a_tab, jnp.concatenate(n1, axis=0), b_tab, jnp.concatenate(r2, axis=0), tie


def _route_head_exact(s1, s2, iota_keys, iota8, iota_cand):
    k = PEER_TOPK
    tt = s1.shape[1]
    n_full = 8
    cand_rows = k + 8 * (n_full - 1) + 8
    a, rank1 = _extract_top(s1, iota_keys, k)
    b, rank2 = _extract_top(s2, iota_keys, k)
    blocks = [a[0:1] + b]
    for i in range(1, n_full):
        blocks.append(jnp.where(iota8 < k // (i + 1), a[i:i + 1] + b[0:8], -jnp.inf))
    blocks.append(a[8:16] + b[0:1])
    cand = jnp.concatenate(blocks, axis=0)
    _, crank = _extract_top(cand, iota_cand, k)
    chosen = crank < k
    z = jnp.sum(jnp.where(chosen, jnp.exp(cand - (a[0:1] + b[0:1])), 0.0), axis=0, keepdims=True)
    chosen_f = jnp.where(chosen, 1.0, 0.0)
    n_lo = jnp.zeros((8, tt), F32)
    n_lo = jnp.where(iota8 == 0, jnp.sum(chosen_f[0:k], axis=0, keepdims=True), n_lo)
    for i in range(1, n_full):
        r0 = k + 8 * (i - 1)
        n_lo = jnp.where(iota8 == i, jnp.sum(chosen_f[r0:r0 + 8], axis=0, keepdims=True), n_lo)
    n_hi = chosen_f[cand_rows - 8:cand_rows]
    n1 = jnp.zeros((N_KEYS, tt), F32)
    for r in range(k):
        n_r = n_lo[r:r + 1] if r < 8 else n_hi[r - 8:r - 7]
        n1 = jnp.where(rank1 == r, n_r, n1)
    return jnp.exp(s1 - a[0:1]) / z, n1, jnp.exp(s2 - b[0:1]), rank2.astype(F32)


def _route_kernel(h_ref, g_ref, wq_ref, keys_ref, hn_ref, a_ref, n1_ref, b_ref, r2_ref, q_s, *, tt):
    hn = _rms(h_ref[...], g_ref[...])
    hn_ref[...] = hn.T.astype(BF16)
    q_s[...] = _dot(hn.astype(BF16), wq_ref[...]).astype(BF16)
    iota8 = lax.broadcasted_iota(I32, (8, 1), 0)

    def scores(h):
        s1 = _dot_nt(keys_ref[h, 0], q_s[:, (2 * h) * N_KEYS:(2 * h + 1) * N_KEYS])
        s2 = _dot_nt(keys_ref[h, 1], q_s[:, (2 * h + 1) * N_KEYS:(2 * h + 2) * N_KEYS])
        return s1, s2

    def store(h, tabs):
        a_ref[h], n1_ref[h] = tabs[0], tabs[1]
        b_ref[h], r2_ref[h] = tabs[2].astype(BF16), tabs[3].astype(BF16)

    tie = jnp.zeros((1, tt), jnp.bool_)
    for h in range(PEER_HEADS):
        tabs = _route_head_fast(*scores(h), iota8)
        store(h, tabs[:4])
        tie = jnp.logical_or(tie, tabs[4])

    @pl.when(jnp.max(jnp.where(tie, 1.0, 0.0)) > 0.0)
    def _():
        iota_keys = lax.broadcasted_iota(I32, (N_KEYS, tt), 0)
        iota_cand = lax.broadcasted_iota(I32, (PEER_TOPK + 8 * 8, tt), 0)
        for h in range(PEER_HEADS):
            store(h, _route_head_exact(*scores(h), iota_keys, iota8, iota_cand))


def peer_route(h2d, g, wq, keys, tt=128):
    t, d = h2d.shape
    tab = jax.ShapeDtypeStruct((PEER_HEADS, N_KEYS, t), F32)
    tab_bf = jax.ShapeDtypeStruct((PEER_HEADS, N_KEYS, t), BF16)
    tab_spec = pl.BlockSpec((PEER_HEADS, N_KEYS, tt), lambda i: (0, 0, i))
    return pl.pallas_call(
        functools.partial(_route_kernel, tt=tt),
        grid=(t // tt,),
        in_specs=[pl.BlockSpec((tt, d), lambda i: (i, 0)),
                  pl.BlockSpec((1, d), lambda i: (0, 0)),
                  pl.BlockSpec(wq.shape, lambda i: (0, 0)),
                  pl.BlockSpec(keys.shape, lambda i: (0, 0, 0, 0))],
        out_specs=[pl.BlockSpec((d, tt), lambda i: (0, i)), tab_spec, tab_spec, tab_spec, tab_spec],
        out_shape=[jax.ShapeDtypeStruct((d, t), BF16), tab, tab, tab_bf, tab_bf],
        scratch_shapes=[pltpu.VMEM((tt, wq.shape[1]), BF16)],
        compiler_params=_params("parallel"),
        name="peer_route",
    )(h2d, g.reshape(1, d), wq, keys)


def _peer_dense_kernel(h_ref, hn_ref, a_ref, n1_ref, b_ref, r2_ref, u_ref, vt_ref, fg_ref, o_ref, acc_ref, p0_ref, p1_ref,
                       *, te, final_norm):
    e = pl.program_id(1)
    n_tiles = pl.num_programs(1) - 1

    @pl.when(e == 0)
    def _():
        acc_ref[...] = jnp.zeros(acc_ref.shape, F32)

    hn = hn_ref[...]
    tt = hn.shape[1]
    keys_per_sub = PEER_SUB // N_KEYS

    def step(p_cur, p_prev):
        @pl.when(e > 0)
        def _():
            acc_ref[...] += _dot(vt_ref[...], p_prev[...])

        @pl.when(e < n_tiles)
        def _():
            for sb in range(te // PEER_SUB):
                r0 = sb * PEER_SUB
                p_cur[r0:r0 + PEER_SUB, :] = jax.nn.gelu(_dot(u_ref[r0:r0 + PEER_SUB, :], hn).astype(BF16))
                for kb in range(keys_per_sub):
                    i1 = e * (te // N_KEYS) + sb * keys_per_sub + kb
                    rows = slice(r0 + kb * N_KEYS, r0 + (kb + 1) * N_KEYS)
                    a_full = [a_ref[h, pl.ds(i1, 1), :].astype(BF16) for h in range(PEER_HEADS)]
                    n_full = [n1_ref[h, pl.ds(i1, 1), :].astype(BF16) for h in range(PEER_HEADS)]
                    for c in range(tt // PEER_COLS):
                        cols = slice(c * PEER_COLS, (c + 1) * PEER_COLS)
                        gate = jnp.zeros((N_KEYS, PEER_COLS), BF16)
                        for h in range(PEER_HEADS):
                            gate = gate + jnp.where(r2_ref[h, :, cols] < n_full[h][:, cols],
                                                    b_ref[h, :, cols] * a_full[h][:, cols], jnp.zeros((), BF16))
                        p_cur[rows, cols] = gate * p_cur[rows, cols]

    @pl.when(e % 2 == 0)
    def _():
        step(p0_ref, p1_ref)

    @pl.when(e % 2 == 1)
    def _():
        step(p1_ref, p0_ref)

    @pl.when(e == n_tiles)
    def _():
        out = h_ref[...] + acc_ref[...].T
        if final_norm:
            out = _rms(out, fg_ref[...])
        o_ref[...] = out


def peer_dense(h2d, hn_t, a, n1, b, r2, u, vt, final_g, final_norm, tt=512, te=1024):
    t, d = h2d.shape
    n_tiles = u.shape[0] // te
    tab_spec = pl.BlockSpec((PEER_HEADS, N_KEYS, tt), lambda i, j: (0, 0, i))
    return pl.pallas_call(
        functools.partial(_peer_dense_kernel, te=te, final_norm=final_norm),
        grid=(t // tt, n_tiles + 1),
        in_specs=[pl.BlockSpec((tt, d), lambda i, j: (i, 0)),
                  pl.BlockSpec((d, tt), lambda i, j: (0, i)),
                  tab_spec, tab_spec, tab_spec, tab_spec,
                  pl.BlockSpec((te, d), lambda i, j: (jnp.minimum(j, n_tiles - 1), 0)),
                  pl.BlockSpec((d, te), lambda i, j: (0, jnp.maximum(j - 1, 0))),
                  pl.BlockSpec((1, d), lambda i, j: (0, 0))],
        out_specs=pl.BlockSpec((tt, d), lambda i, j: (i, 0)),
        out_shape=jax.ShapeDtypeStruct((t, d), F32),
        scratch_shapes=[pltpu.VMEM((d, tt), F32), pltpu.VMEM((te, tt), BF16), pltpu.VMEM((te, tt), BF16)],
        compiler_params=_params("parallel", "arbitrary"),
        name="peer_dense",
    )(h2d, hn_t, a, n1, b, r2, u, vt, final_g.reshape(1, d))


def _pad_odd_w_in(w):
    out = jnp.zeros((w.shape[0], ODD_COLS), w.dtype)
    for name, (src, width) in ODD_SRC.items():
        out = lax.dynamic_update_slice(out, w[:, src:src + width], (0, ODD_DST[name]))
    return out


def _block_diag_uv(w_uv):
    out = jnp.zeros((C_HEADS * C_LATENT, C_HEADS * C_HEAD_DIM), w_uv.dtype)
    for h in range(C_HEADS):
        out = lax.dynamic_update_slice(out, w_uv[h], (h * C_LATENT, h * C_HEAD_DIM))
    return out


def kernel(x, mem, mix_norm, even_w_in, even_conv_w, even_pool_w, even_pool_scale, even_w_out, odd_w_in, odd_kv_norm, odd_w_uv, odd_hg_norm, odd_w_out, hgrn_gamma, rel_bias, mem_norm, xattn_norm, xattn_wq, xattn_wkv, xattn_wo, peer_norm, peer_wq, peer_keys, peer_u, peer_v, final_norm):
    bsz, s, d = x.shape
    t = bsz * s
    depth = mix_norm.shape[0]
    mem2d = mem.reshape(-1, d)
    h = x
    for l in range(depth):
        j = l // 2
        if l % 2 == 0:
            h = even_mixer(h, mix_norm[l], even_w_in[j].astype(BF16), even_conv_w[j].reshape(3, A_WIDTH),
                           even_pool_w[j].astype(BF16), even_pool_scale[j].reshape(1, -1),
                           even_w_out[j].astype(BF16))
        else:
            proj = norm_matmul(h.reshape(t, d), mix_norm[l], _pad_odd_w_in(odd_w_in[j]).astype(BF16), F32)
            proj = proj.reshape(bsz, s, ODD_COLS)
            y_c = dsa_attention(proj, rel_bias, odd_kv_norm[j], _block_diag_uv(odd_w_uv[j]).astype(BF16))
            y_d = hgrn2(proj, hgrn_gamma, odd_hg_norm[j], l)
            h = outproj_residual(h.reshape(t, d), y_c.reshape(t, -1), y_d.reshape(t, -1),
                                 odd_w_out[j].astype(BF16)).reshape(bsz, s, d)
        kv = norm_matmul(mem2d, mem_norm, xattn_wkv[l].astype(BF16), BF16).reshape(bsz, -1, 2 * d)
        h = memory_cross_attn(h, xattn_norm[l], xattn_wq[l].astype(BF16), kv, xattn_wo[l].astype(BF16))
        h2d = h.reshape(t, d)
        hn, a, n1, b, r2 = peer_route(h2d, peer_norm[l], peer_wq[l].astype(BF16), peer_keys[l].astype(BF16))
        h = peer_dense(h2d, hn, a, n1, b, r2, peer_u[l].astype(BF16), peer_v[l].T.astype(BF16),
                       final_norm, l == depth - 1).reshape(bsz, s, d)
    return h
```

```python
import functools
import math

import numpy as np
import jax
import jax.numpy as jnp
from jax import lax
from jax.experimental import pallas as pl
from jax.experimental.pallas import tpu as pltpu

F32 = jnp.float32
BF16 = jnp.bfloat16
I32 = jnp.int32

EPS = 1e-6
LANES = 128
INT_MIN = -(2 ** 31)
VMEM_LIMIT = 56 * 1024 * 1024

POOL_WINDOWS = (2, 4, 8, 16)
HALO = 16
A_WIDTH = 512
POOL_GROUP = 128

C_HEADS = 8
C_LATENT = 128
C_HEAD_DIM = 64
IDX_HEADS = 8
IDX_DIM = 64
DSA_TOPK_MAX = 256
Q_BLOCK = 128
DSA_TIERS = 4
REL_BUCKETS = 32
REL_MAX_DIST = 128

D_HEADS = 4
D_KEY = 128
HG_CHUNK = 128
HG_SUB = 16

X_HEADS = 4
X_HEAD_DIM = 256

N_KEYS = 128
PEER_HEADS = 8
PEER_TOPK = 16
NO_RANK = 127
PEER_SUB = 256
PEER_COLS = 256

ODD_COLS = 4096
ODD_SRC = dict(q_lat=(0, 1024), c_kv=(1024, 128), iq=(1152, 512), ik=(1664, 64), iw=(1728, 8),
               hq=(1736, 512), hf=(2248, 512), hi=(2760, 512), hg=(3272, 512))
ODD_DST = dict(q_lat=0, iq=1024, c_kv=1536, ik=1664, iw=1792, hq=2048, hf=2560, hi=3072, hg=3584)

NT_DIMS = (((1,), (1,)), ((), ()))


def _params(*sem):
    return pltpu.CompilerParams(dimension_semantics=sem, vmem_limit_bytes=VMEM_LIMIT)


def _rms(x, g):
    return x * lax.rsqrt(jnp.mean(x * x, axis=-1, keepdims=True) + EPS) * g


def _dot(a, b):
    return jnp.dot(a, b, preferred_element_type=F32)


def _dot_nt(a, b):
    return lax.dot_general(a, b, NT_DIMS, preferred_element_type=F32)


def _norm_matmul_kernel(x_ref, g_ref, w_ref, o_ref):
    y = _rms(x_ref[...], g_ref[...])
    o_ref[...] = _dot(y.astype(BF16), w_ref[...]).astype(o_ref.dtype)


def norm_matmul(x, g, w, out_dtype, tm=256):
    m, d = x.shape
    n = w.shape[1]
    return pl.pallas_call(
        _norm_matmul_kernel,
        grid=(m // tm,),
        in_specs=[pl.BlockSpec((tm, d), lambda i: (i, 0)),
                  pl.BlockSpec((1, d), lambda i: (0, 0)),
                  pl.BlockSpec((d, n), lambda i: (0, 0))],
        out_specs=pl.BlockSpec((tm, n), lambda i: (i, 0)),
        out_shape=jax.ShapeDtypeStruct((m, n), out_dtype),
        compiler_params=_params("parallel"),
        name="norm_matmul",
    )(x, g.reshape(1, d), w)


def _even_kernel(x_ref, g_ref, win_ref, cw_ref, pw_ref, ps_ref, wout_ref, o_ref, vbuf, pbuf, *, ts):
    j = pl.program_id(1)

    @pl.when(j == 0)
    def _():
        vbuf[0:HALO, :] = jnp.zeros((HALO, A_WIDTH), F32)
        pbuf[0:HALO, :] = jnp.zeros((HALO, A_WIDTH), F32)

    x = x_ref[0]
    proj = _dot(_rms(x, g_ref[...]).astype(BF16), win_ref[...])
    u = proj[:, 0:512]
    gate_c = proj[:, 512:1024]
    gate_b = proj[:, 1024:1536]
    pv = proj[:, 1536:2048]
    vbuf[HALO:HALO + ts, :] = gate_c * u
    pbuf[HALO:HALO + ts, :] = pv

    cw = cw_ref[...]
    z = (cw[2:3] * vbuf[HALO:HALO + ts, :] + cw[1:2] * vbuf[HALO - 1:HALO - 1 + ts, :]
         + cw[0:1] * vbuf[HALO - 2:HALO - 2 + ts, :])
    y_a = gate_b * z

    pos = j * ts + lax.broadcasted_iota(I32, (ts, 1), 0)
    y_b = []
    for gi, w in enumerate(POOL_WINDOWS):
        c0 = gi * POOL_GROUP
        acc = pbuf[HALO:HALO + ts, c0:c0 + POOL_GROUP]
        for k in range(1, w):
            acc = acc + pbuf[HALO - k:HALO - k + ts, c0:c0 + POOL_GROUP]
        cnt = jnp.minimum(pos + 1, w).astype(F32)
        pooled = acc / cnt - pv[:, c0:c0 + POOL_GROUP]
        y_b.append(_dot(pooled.astype(BF16), pw_ref[gi]))
    y_b = jnp.concatenate(y_b, axis=-1) * ps_ref[...]

    o_ref[0] = (x + _dot(y_a.astype(BF16), wout_ref[0:A_WIDTH, :])
                + _dot(y_b.astype(BF16), wout_ref[A_WIDTH:2 * A_WIDTH, :]))

    vbuf[0:HALO, :] = vbuf[ts:ts + HALO, :]
    pbuf[0:HALO, :] = pbuf[ts:ts + HALO, :]


def even_mixer(h, g, w_in, conv_w, pool_w, pool_scale, w_out, ts=512):
    b, s, d = h.shape
    return pl.pallas_call(
        functools.partial(_even_kernel, ts=ts),
        grid=(b, s // ts),
        in_specs=[pl.BlockSpec((1, ts, d), lambda i, j: (i, j, 0)),
                  pl.BlockSpec((1, d), lambda i, j: (0, 0)),
                  pl.BlockSpec(w_in.shape, lambda i, j: (0, 0)),
                  pl.BlockSpec(conv_w.shape, lambda i, j: (0, 0)),
                  pl.BlockSpec(pool_w.shape, lambda i, j: (0, 0, 0)),
                  pl.BlockSpec(pool_scale.shape, lambda i, j: (0, 0)),
                  pl.BlockSpec(w_out.shape, lambda i, j: (0, 0))],
        out_specs=pl.BlockSpec((1, ts, d), lambda i, j: (i, j, 0)),
        out_shape=jax.ShapeDtypeStruct(h.shape, F32),
        scratch_shapes=[pltpu.VMEM((HALO + ts, A_WIDTH), F32), pltpu.VMEM((HALO + ts, A_WIDTH), F32)],
        compiler_params=_params("parallel", "arbitrary"),
        name="even_mixer",
    )(h, g.reshape(1, d), w_in, conv_w, pool_w, pool_scale, w_out)


def _t5_bucket_table():
    t = np.arange(Q_BLOCK)[:, None]
    s = np.arange(Q_BLOCK)[None, :]
    dist = np.concatenate([t - s, t - s + Q_BLOCK], axis=1)
    max_exact = REL_BUCKETS // 2
    d = np.maximum(dist, 0)
    log_ratio = (np.log(np.maximum(d, 1).astype(np.float32) / np.float32(max_exact))
                 / np.float32(math.log(REL_MAX_DIST / max_exact)))
    large = max_exact + (log_ratio * np.float32(REL_BUCKETS - max_exact)).astype(np.int32)
    large = np.minimum(large, REL_BUCKETS - 1)
    return np.where(d < max_exact, d, large).astype(np.int32)


def _dsa_kernel(rb_ref, bk_ref, q_ref, iq_ref, iw_ref, ckv_ref, ik_ref, kvn_ref, wuv_ref, o_ref,
                ckv_s, ik_s, bias_s, key_s, sel_s, lg_s, *, s_len, topk):
    qi = pl.program_id(1)

    @pl.when(qi == 0)
    def _():
        ckv_s[...] = _rms(ckv_ref[0], kvn_ref[...]).astype(BF16)
        ik_s[...] = ik_ref[0][:, 0:IDX_DIM].astype(BF16)
        bk = bk_ref[...]
        for h in range(C_HEADS):
            tile = jnp.zeros(bk.shape, F32)
            for k in range(REL_BUCKETS):
                tile = jnp.where(bk == k, rb_ref[k, h], tile)
            bias_s[h] = tile

    n_q = s_len // Q_BLOCK
    tiers = DSA_TIERS if n_q % DSA_TIERS == 0 else 1
    for tier in range(tiers):
        lo, hi = tier * n_q // tiers, (tier + 1) * n_q // tiers

        @pl.when(jnp.logical_and(qi >= lo, qi < hi))
        def _(s_eff=hi * Q_BLOCK):
            _dsa_block(rb_ref, q_ref, iq_ref, iw_ref, wuv_ref, o_ref, ckv_s, ik_s, bias_s, key_s, sel_s, lg_s,
                       qi=qi, s_eff=s_eff, topk=topk)


def _dsa_block(rb_ref, q_ref, iq_ref, iw_ref, wuv_ref, o_ref, ckv_s, ik_s, bias_s, key_s, sel_s, lg_s, *, qi, s_eff, topk):
    scale = C_LATENT ** -0.5
    far_bucket = REL_BUCKETS - 1
    t_idx = qi * Q_BLOCK + lax.broadcasted_iota(I32, (Q_BLOCK, 1), 0)
    s_idx = lax.broadcasted_iota(I32, (1, s_eff), 1)
    causal = s_idx <= t_idx
    keys = slice(0, s_eff)

    iq = iq_ref[0]
    iw = iw_ref[0] * (IDX_HEADS ** -0.5 * IDX_DIM ** -0.5)
    ik = ik_s[keys, :]
    score = jnp.zeros((Q_BLOCK, s_eff), F32)
    for h in range(IDX_HEADS):
        s_h = _dot_nt(iq[:, h * IDX_DIM:(h + 1) * IDX_DIM].astype(BF16), ik)
        score = score + jnp.maximum(s_h, 0.0) * iw[:, h:h + 1]

    bits = lax.bitcast_convert_type(score + 0.0, I32)
    key = jnp.where(bits < 0, bits ^ jnp.int32(0x7FFFFFFF), bits)
    key_s[:, keys] = jnp.where(causal, key, jnp.int32(INT_MIN))

    def count(mask):
        return jnp.sum(jnp.where(mask, 1.0, 0.0), axis=1, keepdims=True)

    def thr_step(i, ans):
        cand = ans + lax.shift_left(jnp.int32(1), 31 - i)
        return jnp.where(count(key_s[:, keys] >= cand) >= topk, cand, ans)

    thr = lax.fori_loop(0, 32, thr_step, jnp.full((Q_BLOCK, 1), INT_MIN, I32))
    key = key_s[:, keys]
    gt = key > thr
    eq = key == thr
    need = topk - count(gt)
    tie_rows = jnp.logical_and(count(eq) > need, thr != jnp.int32(INT_MIN))
    any_tie = jnp.max(jnp.where(tie_rows, 1.0, 0.0)) > 0.0
    cut_bits = s_eff.bit_length()

    def tie_cut():
        def step(i, cut):
            cand = cut + lax.shift_left(jnp.int32(1), cut_bits - 1 - i)
            c = count(jnp.logical_and(key_s[:, keys] == thr, s_idx < cand))
            return jnp.where(c <= need, cand, cut)
        return lax.fori_loop(0, cut_bits, step, jnp.zeros((Q_BLOCK, 1), I32))

    cut = lax.cond(any_tie, tie_cut, lambda: jnp.full((Q_BLOCK, 1), s_eff, I32))
    sel = jnp.logical_and(causal, jnp.logical_or(gt, jnp.logical_and(eq, s_idx < cut)))
    sel_s[:, keys] = jnp.where(sel, 0.0, -jnp.inf)

    off = pl.multiple_of(qi * Q_BLOCK, Q_BLOCK)
    off_prev = pl.multiple_of(jnp.maximum(qi - 1, 0) * Q_BLOCK, Q_BLOCK)
    ckv = ckv_s[keys, :]
    lats = []
    for h in range(C_HEADS):
        qh = (q_ref[0][:, h * C_LATENT:(h + 1) * C_LATENT] * scale).astype(BF16)
        far = rb_ref[far_bucket, h]
        lg_s[:, keys] = _dot_nt(qh, ckv) + sel_s[:, keys]
        lg_s[:, pl.ds(off, Q_BLOCK)] = (_dot_nt(qh, ckv_s[pl.ds(off, Q_BLOCK), :])
                                        + (bias_s[h][:, 0:Q_BLOCK] - far) + sel_s[:, pl.ds(off, Q_BLOCK)])

        @pl.when(qi > 0)
        def _():
            lg_s[:, pl.ds(off_prev, Q_BLOCK)] = (_dot_nt(qh, ckv_s[pl.ds(off_prev, Q_BLOCK), :])
                                                 + (bias_s[h][:, Q_BLOCK:2 * Q_BLOCK] - far)
                                                 + sel_s[:, pl.ds(off_prev, Q_BLOCK)])

        logits = lg_s[:, keys]
        p = jnp.exp(logits - jnp.max(logits, axis=1, keepdims=True))
        lat = _dot(p.astype(BF16), ckv) / jnp.sum(p, axis=1, keepdims=True)
        lats.append(lat.astype(BF16))
    o_ref[0] = _dot(jnp.concatenate(lats, axis=-1), wuv_ref[...])


def dsa_attention(proj, rel_bias, kv_norm, wuv_bd):
    b, s, _ = proj.shape
    topk = min(DSA_TOPK_MAX, s // 4)
    bk = jnp.asarray(_t5_bucket_table())
    col = lambda name, width: ODD_DST[name] // width
    return pl.pallas_call(
        functools.partial(_dsa_kernel, s_len=s, topk=topk),
        grid=(b, s // Q_BLOCK),
        in_specs=[pl.BlockSpec(memory_space=pltpu.SMEM),
                  pl.BlockSpec(bk.shape, lambda i, j: (0, 0)),
                  pl.BlockSpec((1, Q_BLOCK, 1024), lambda i, j: (i, j, col("q_lat", 1024))),
                  pl.BlockSpec((1, Q_BLOCK, 512), lambda i, j: (i, j, col("iq", 512))),
                  pl.BlockSpec((1, Q_BLOCK, LANES), lambda i, j: (i, j, col("iw", LANES))),
                  pl.BlockSpec((1, s, LANES), lambda i, j: (i, 0, col("c_kv", LANES))),
                  pl.BlockSpec((1, s, LANES), lambda i, j: (i, 0, col("ik", LANES))),
                  pl.BlockSpec((1, C_LATENT), lambda i, j: (0, 0)),
                  pl.BlockSpec(wuv_bd.shape, lambda i, j: (0, 0))],
        out_specs=pl.BlockSpec((1, Q_BLOCK, C_HEADS * C_HEAD_DIM), lambda i, j: (i, j, 0)),
        out_shape=jax.ShapeDtypeStruct((b, s, C_HEADS * C_HEAD_DIM), F32),
        scratch_shapes=[pltpu.VMEM((s, C_LATENT), BF16),
                        pltpu.VMEM((s, IDX_DIM), BF16),
                        pltpu.VMEM((C_HEADS, Q_BLOCK, 2 * Q_BLOCK), F32),
                        pltpu.VMEM((Q_BLOCK, s), I32),
                        pltpu.VMEM((Q_BLOCK, s), F32),
                        pltpu.VMEM((Q_BLOCK, s), F32)],
        compiler_params=_params("parallel", "arbitrary"),
        name="dsa_attention",
    )(rel_bias, bk, proj, proj, proj, proj, proj, kv_norm.reshape(1, C_LATENT), wuv_bd)


def _hgrn_kernel(gam_ref, nrm_ref, q_ref, f_ref, i_ref, g_ref, o_ref, st_ref, *, layer):
    c = pl.program_id(1)
    ch = HG_CHUNK

    @pl.when(c == 0)
    def _():
        st_ref[...] = jnp.zeros(st_ref.shape, F32)

    gam = gam_ref[...]
    pe = jnp.exp(gam - jnp.max(gam, axis=0, keepdims=True))
    pe = pe / jnp.sum(pe, axis=0, keepdims=True)
    lb = jnp.zeros((1, gam.shape[1]), F32)
    for i in range(layer):
        lb = lb + pe[i:i + 1]

    f = lb + (1.0 - lb) * jax.nn.sigmoid(f_ref[0])
    kk = 1.0 - f
    a = jnp.log(f)
    row = lax.broadcasted_iota(I32, (ch, 1), 0)
    sh = 1
    while sh < ch:
        a = a + jnp.where(row >= sh, pltpu.roll(a, sh, axis=0), 0.0)
        sh *= 2
    q = q_ref[0]
    inp = i_ref[0]
    a_end = a[ch - 1:ch, :]
    qe = q * jnp.exp(a)
    kd_end = kk * jnp.exp(a_end - a)
    trow = lax.broadcasted_iota(I32, (HG_SUB, 1), 0)

    outs = []
    for h in range(D_HEADS):
        sl = slice(h * D_KEY, (h + 1) * D_KEY)
        ah, qh, kh, ih = a[:, sl], q[:, sl], kk[:, sl], inp[:, sl]
        ih_bf = ih.astype(BF16)
        st_t = st_ref[h]
        o = _dot_nt(qe[:, sl].astype(BF16), st_t.astype(BF16))
        blocks = []
        for i in range(ch // HG_SUB):
            r0 = i * HG_SUB
            ai, qi, ki, ii = ah[r0:r0 + HG_SUB], qh[r0:r0 + HG_SUB], kh[r0:r0 + HG_SUB], ih[r0:r0 + HG_SUB]
            acc = jnp.zeros((HG_SUB, D_KEY), F32)
            if i > 0:
                r_i = ah[r0 - 1:r0]
                qd = qi * jnp.exp(ai - r_i)
                kd = kh[0:r0] * jnp.exp(r_i - ah[0:r0])
                s_off = _dot_nt(qd.astype(BF16), kd.astype(BF16))
                acc = _dot(s_off.astype(BF16), ih_bf[0:r0])
            for s in range(HG_SUB):
                e = jnp.exp(jnp.where(trow >= s, ai - ai[s:s + 1], -jnp.inf))
                sc = jnp.sum(qi * ki[s:s + 1] * e, axis=1, keepdims=True)
                acc = acc + sc * ii[s:s + 1]
            blocks.append(acc)
        o = o + jnp.concatenate(blocks, axis=0)
        st_ref[h] = st_t * jnp.exp(a_end[:, sl]) + _dot(ih.T.astype(BF16), kd_end[:, sl].astype(BF16))
        outs.append(_rms(o, nrm_ref[:, sl]))
    o_ref[0] = jnp.concatenate(outs, axis=-1) * jax.nn.silu(g_ref[0])


def hgrn2(proj, gamma, hg_norm, layer):
    b, s, _ = proj.shape
    w = D_HEADS * D_KEY
    col = lambda name: ODD_DST[name] // w
    spec = lambda name: pl.BlockSpec((1, HG_CHUNK, w), lambda i, j: (i, j, col(name)))
    return pl.pallas_call(
        functools.partial(_hgrn_kernel, layer=layer),
        grid=(b, s // HG_CHUNK),
        in_specs=[pl.BlockSpec(gamma.shape, lambda i, j: (0, 0)),
                  pl.BlockSpec((1, w), lambda i, j: (0, 0)),
                  spec("hq"), spec("hf"), spec("hi"), spec("hg")],
        out_specs=pl.BlockSpec((1, HG_CHUNK, w), lambda i, j: (i, j, 0)),
        out_shape=jax.ShapeDtypeStruct((b, s, w), F32),
        scratch_shapes=[pltpu.VMEM((D_HEADS, D_KEY, D_KEY), F32)],
        compiler_params=_params("parallel", "arbitrary"),
        name="hgrn2",
    )(gamma, hg_norm.reshape(1, w), proj, proj, proj, proj)


def _outproj_kernel(h_ref, a_ref, b_ref, w_ref, o_ref):
    ka = a_ref.shape[1]
    o_ref[...] = (h_ref[...] + _dot(a_ref[...].astype(BF16), w_ref[0:ka, :])
                  + _dot(b_ref[...].astype(BF16), w_ref[ka:, :]))


def outproj_residual(h, a, b, w, tm=512):
    m, d = h.shape
    return pl.pallas_call(
        _outproj_kernel,
        grid=(m // tm,),
        in_specs=[pl.BlockSpec((tm, d), lambda i: (i, 0)),
                  pl.BlockSpec((tm, a.shape[1]), lambda i: (i, 0)),
                  pl.BlockSpec((tm, b.shape[1]), lambda i: (i, 0)),
                  pl.BlockSpec(w.shape, lambda i: (0, 0))],
        out_specs=pl.BlockSpec((tm, d), lambda i: (i, 0)),
        out_shape=jax.ShapeDtypeStruct((m, d), F32),
        compiler_params=_params("parallel"),
        name="outproj_residual",
    )(h, a, b, w)


def _xattn_kernel(h_ref, g_ref, wq_ref, kv_ref, wo_ref, o_ref):
    x = h_ref[0]
    q = _dot(_rms(x, g_ref[...]).astype(BF16), wq_ref[...])
    kv = kv_ref[0]
    dm = X_HEADS * X_HEAD_DIM
    outs = []
    for h in range(X_HEADS):
        sl = slice(h * X_HEAD_DIM, (h + 1) * X_HEAD_DIM)
        logits = _dot_nt(q[:, sl].astype(BF16), kv[:, sl]) * (X_HEAD_DIM ** -0.5)
        p = jnp.exp(logits - jnp.max(logits, axis=1, keepdims=True))
        p = p / jnp.sum(p, axis=1, keepdims=True)
        outs.append(_dot(p.astype(BF16), kv[:, dm + h * X_HEAD_DIM:dm + (h + 1) * X_HEAD_DIM]).astype(BF16))
    o_ref[0] = x + _dot(jnp.concatenate(outs, axis=-1), wo_ref[...])


def memory_cross_attn(h, g, wq, kv, wo, ts=512):
    b, s, d = h.shape
    return pl.pallas_call(
        _xattn_kernel,
        grid=(b, s // ts),
        in_specs=[pl.BlockSpec((1, ts, d), lambda i, j: (i, j, 0)),
                  pl.BlockSpec((1, d), lambda i, j: (0, 0)),
                  pl.BlockSpec(wq.shape, lambda i, j: (0, 0)),
                  pl.BlockSpec((1,) + kv.shape[1:], lambda i, j: (i, 0, 0)),
                  pl.BlockSpec(wo.shape, lambda i, j: (0, 0))],
        out_specs=pl.BlockSpec((1, ts, d), lambda i, j: (i, j, 0)),
        out_shape=jax.ShapeDtypeStruct(h.shape, F32),
        compiler_params=_params("parallel", "parallel"),
        name="memory_cross_attn",
    )(h, g.reshape(1, d), wq, kv, wo)


def _extract_top(work, iota, rounds):
    rows = work.shape[0]
    rank = jnp.full(work.shape, NO_RANK, I32)
    vals = jnp.zeros((rounds, work.shape[1]), F32)
    riota = lax.broadcasted_iota(I32, (rounds, 1), 0)
    for r in range(rounds):
        m = jnp.max(work, axis=0, keepdims=True)
        first = jnp.min(jnp.where(work == m, iota, rows), axis=0, keepdims=True)
        hit = iota == first
        rank = jnp.where(hit, r, rank)
        work = jnp.where(hit, -jnp.inf, work)
        vals = jnp.where(riota == r, m, vals)
    return vals, rank


def _batcher_pairs(n):
    pairs = []
    p = 1
    while p < n:
        k = p
        while k >= 1:
            for j in range(k % p, n - k, 2 * k):
                for i in range(min(k, n - j - k)):
                    if (i + j) // (2 * p) == (i + j + k) // (2 * p):
                        pairs.append((i + j, i + j + k))
            k //= 2
        p *= 2
    return pairs


_SORT16 = _batcher_pairs(PEER_TOPK)


def _compare_exchange(w, i, j):
    a, b = w[i], w[j]
    if b is None:
        return
    if a is None:
        w[i], w[j] = b, None
        return
    w[i], w[j] = jnp.maximum(a, b), jnp.minimum(a, b)


def _sorted_top(slabs):
    k = PEER_TOPK
    w = list(slabs) + [None] * (k - len(slabs))
    for i, j in _SORT16:
        _compare_exchange(w, i, j)
    for shift in (4, 2, 1):
        r = [None if x is None else pltpu.roll(x, shift, axis=0) for x in w]
        t = []
        for i in range(k):
            a, b = w[i], r[k - 1 - i]
            t.append(b if a is None else a if b is None else jnp.maximum(a, b))
        w = t
        d = k // 2
        while d >= 1:
            for i in range(k):
                if i & d == 0:
                    _compare_exchange(w, i, i + d)
            d //= 2
    return w


def _slabs(x):
    return [x[8 * j:8 * j + 8] for j in range(x.shape[0] // 8)]


def _count_ge(slabs, thr):
    acc = None
    for x in slabs:
        c = jnp.where(x >= thr, 1.0, 0.0)
        acc = c if acc is None else acc + c
    return jnp.sum(acc, axis=0, keepdims=True)


def _route_head_fast(s1, s2, iota8):
    k = PEER_TOPK
    c1, c2 = _slabs(s1), _slabs(s2)
    a, b = _sorted_top(c1), _sorted_top(c2)
    tie = jnp.logical_or(_count_ge(c1, a[k - 1]) != k, _count_ge(c2, b[k - 1]) != k)
    for i in range(k - 1):
        tie = jnp.logical_or(tie, jnp.logical_or(a[i][0:1] <= a[i + 1][0:1], b[i][0:1] <= b[i + 1][0:1]))

    def stack8(vals):
        out = vals[0]
        for r in range(1, 8):
            out = jnp.where(iota8 == r, vals[r], out)
        return out

    a_hi, b_lo, b_hi = stack8(a[8:]), stack8(b[:8]), stack8(b[8:])
    cand = [a[0] + b_lo, a[0] + b_hi]
    for i in range(1, 8):
        cand.append(jnp.where(iota8 < k // (i + 1), a[i] + b_lo, -jnp.inf))
    cand.append(a_hi + b[0])
    thr = _sorted_top(cand)[k - 1]
    tie = jnp.logical_or(tie, _count_ge(cand, thr) != k)
    chosen = [jnp.where(x >= thr, 1.0, 0.0) for x in cand]
    top = a[0] + b[0]
    z = None
    for x, m in zip(cand, chosen):
        e = m * jnp.exp(x - top)
        z = e if z is None else z + e
    z = jnp.sum(z, axis=0, keepdims=True)
    n = [jnp.sum(chosen[0] + chosen[1], axis=0, keepdims=True)]
    n += [jnp.sum(chosen[i + 1], axis=0, keepdims=True) for i in range(1, 8)]
    n += [chosen[9][r:r + 1] for r in range(8)]
    n = [jnp.broadcast_to(x, (8, x.shape[1])) for x in n]
    n1, r2 = [], []
    for x1, x2 in zip(c1, c2):
        acc_n = jnp.zeros(x1.shape, F32)
        acc_r = jnp.full(x2.shape, float(NO_RANK), F32)
        for r in range(k - 1, -1, -1):
            acc_n = jnp.where(x1 >= a[r], n[r], acc_n)
            acc_r = jnp.where(x2 >= b[r], float(r), acc_r)
        n1.append(acc_n)
        r2.append(acc_r)
    a_tab = jnp.exp(s1 - a[0][0:1]) / z
    b_tab = jnp.exp(s2 - b[0][0:1])
    return a_tab, jnp.concatenate(n1, axis=0), b_tab, jnp.concatenate(r2, axis=0), tie


def _route_head_exact(s1, s2, iota_keys, iota8, iota_cand):
    k = PEER_TOPK
    tt = s1.shape[1]
    n_full = 8
    cand_rows = k + 8 * (n_full - 1) + 8
    a, rank1 = _extract_top(s1, iota_keys, k)
    b, rank2 = _extract_top(s2, iota_keys, k)
    blocks = [a[0:1] + b]
    for i in range(1, n_full):
        blocks.append(jnp.where(iota8 < k // (i + 1), a[i:i + 1] + b[0:8], -jnp.inf))
    blocks.append(a[8:16] + b[0:1])
    cand = jnp.concatenate(blocks, axis=0)
    _, crank = _extract_top(cand, iota_cand, k)
    chosen = crank < k
    z = jnp.sum(jnp.where(chosen, jnp.exp(cand - (a[0:1] + b[0:1])), 0.0), axis=0, keepdims=True)
    chosen_f = jnp.where(chosen, 1.0, 0.0)
    n_lo = jnp.zeros((8, tt), F32)
    n_lo = jnp.where(iota8 == 0, jnp.sum(chosen_f[0:k], axis=0, keepdims=True), n_lo)
    for i in range(1, n_full):
        r0 = k + 8 * (i - 1)
        n_lo = jnp.where(iota8 == i, jnp.sum(chosen_f[r0:r0 + 8], axis=0, keepdims=True), n_lo)
    n_hi = chosen_f[cand_rows - 8:cand_rows]
    n1 = jnp.zeros((N_KEYS, tt), F32)
    for r in range(k):
        n_r = n_lo[r:r + 1] if r < 8 else n_hi[r - 8:r - 7]
        n1 = jnp.where(rank1 == r, n_r, n1)
    return jnp.exp(s1 - a[0:1]) / z, n1, jnp.exp(s2 - b[0:1]), rank2.astype(F32)


def _route_kernel(h_ref, g_ref, wq_ref, keys_ref, hn_ref, a_ref, n1_ref, b_ref, r2_ref, q_s, *, tt):
    hn = _rms(h_ref[...], g_ref[...])
    hn_ref[...] = hn.T.astype(BF16)
    q_s[...] = _dot(hn.astype(BF16), wq_ref[...]).astype(BF16)
    iota8 = lax.broadcasted_iota(I32, (8, 1), 0)

    def scores(h):
        s1 = _dot_nt(keys_ref[h, 0], q_s[:, (2 * h) * N_KEYS:(2 * h + 1) * N_KEYS])
        s2 = _dot_nt(keys_ref[h, 1], q_s[:, (2 * h + 1) * N_KEYS:(2 * h + 2) * N_KEYS])
        return s1, s2

    def store(h, tabs):
        a_ref[h], n1_ref[h] = tabs[0], tabs[1]
        b_ref[h], r2_ref[h] = tabs[2].astype(BF16), tabs[3].astype(BF16)

    tie = jnp.zeros((1, tt), jnp.bool_)
    for h in range(PEER_HEADS):
        tabs = _route_head_fast(*scores(h), iota8)
        store(h, tabs[:4])
        tie = jnp.logical_or(tie, tabs[4])

    @pl.when(jnp.max(jnp.where(tie, 1.0, 0.0)) > 0.0)
    def _():
        iota_keys = lax.broadcasted_iota(I32, (N_KEYS, tt), 0)
        iota_cand = lax.broadcasted_iota(I32, (PEER_TOPK + 8 * 8, tt), 0)
        for h in range(PEER_HEADS):
            store(h, _route_head_exact(*scores(h), iota_keys, iota8, iota_cand))


def peer_route(h2d, g, wq, keys, tt=128):
    t, d = h2d.shape
    tab = jax.ShapeDtypeStruct((PEER_HEADS, N_KEYS, t), F32)
    tab_bf = jax.ShapeDtypeStruct((PEER_HEADS, N_KEYS, t), BF16)
    tab_spec = pl.BlockSpec((PEER_HEADS, N_KEYS, tt), lambda i: (0, 0, i))
    return pl.pallas_call(
        functools.partial(_route_kernel, tt=tt),
        grid=(t // tt,),
        in_specs=[pl.BlockSpec((tt, d), lambda i: (i, 0)),
                  pl.BlockSpec((1, d), lambda i: (0, 0)),
                  pl.BlockSpec(wq.shape, lambda i: (0, 0)),
                  pl.BlockSpec(keys.shape, lambda i: (0, 0, 0, 0))],
        out_specs=[pl.BlockSpec((d, tt), lambda i: (0, i)), tab_spec, tab_spec, tab_spec, tab_spec],
        out_shape=[jax.ShapeDtypeStruct((d, t), BF16), tab, tab, tab_bf, tab_bf],
        scratch_shapes=[pltpu.VMEM((tt, wq.shape[1]), BF16)],
        compiler_params=_params("parallel"),
        name="peer_route",
    )(h2d, g.reshape(1, d), wq, keys)


def _peer_dense_kernel(h_ref, hn_ref, a_ref, n1_ref, b_ref, r2_ref, u_ref, vt_ref, fg_ref, o_ref, acc_ref, p_ref,
                       *, te, final_norm):
    e = pl.program_id(1)

    @pl.when(e == 0)
    def _():
        acc_ref[...] = jnp.zeros(acc_ref.shape, F32)

    hn = hn_ref[...]
    tt = hn.shape[1]
    keys_per_sub = PEER_SUB // N_KEYS
    n_sub = te // PEER_SUB
    d_rows = acc_ref.shape[0] // n_sub
    for sb in range(n_sub):
        r0 = sb * PEER_SUB
        p_ref[r0:r0 + PEER_SUB, :] = jax.nn.gelu(_dot(u_ref[r0:r0 + PEER_SUB, :], hn).astype(BF16))
        for kb in range(keys_per_sub):
            i1 = e * (te // N_KEYS) + sb * keys_per_sub + kb
            rows = slice(r0 + kb * N_KEYS, r0 + (kb + 1) * N_KEYS)
            a_full = [a_ref[h, pl.ds(i1, 1), :].astype(BF16) for h in range(PEER_HEADS)]
            n_full = [n1_ref[h, pl.ds(i1, 1), :].astype(BF16) for h in range(PEER_HEADS)]
            for c in range(tt // PEER_COLS):
                cols = slice(c * PEER_COLS, (c + 1) * PEER_COLS)
                gate = jnp.zeros((N_KEYS, PEER_COLS), BF16)
                for h in range(PEER_HEADS):
                    gate = gate + jnp.where(r2_ref[h, :, cols] < n_full[h][:, cols],
                                            b_ref[h, :, cols] * a_full[h][:, cols], jnp.zeros((), BF16))
                p_ref[rows, cols] = gate * p_ref[rows, cols]
    for sb in range(n_sub):
        out_rows = slice(sb * d_rows, (sb + 1) * d_rows)
        acc_ref[out_rows, :] += _dot(vt_ref[out_rows, :], p_ref[...])

    @pl.when(e == pl.num_programs(1) - 1)
    def _():
        out = h_ref[...] + acc_ref[...].T
        if final_norm:
            out = _rms(out, fg_ref[...])
        o_ref[...] = out


def peer_dense(h2d, hn_t, a, n1, b, r2, u, vt, final_g, final_norm, tt=512, te=1024):
    t, d = h2d.shape
    tab_spec = pl.BlockSpec((PEER_HEADS, N_KEYS, tt), lambda i, j: (0, 0, i))
    return pl.pallas_call(
        functools.partial(_peer_dense_kernel, te=te, final_norm=final_norm),
        grid=(t // tt, u.shape[0] // te),
        in_specs=[pl.BlockSpec((tt, d), lambda i, j: (i, 0)),
                  pl.BlockSpec((d, tt), lambda i, j: (0, i)),
                  tab_spec, tab_spec, tab_spec, tab_spec,
                  pl.BlockSpec((te, d), lambda i, j: (j, 0)),
                  pl.BlockSpec((d, te), lambda i, j: (0, j)),
                  pl.BlockSpec((1, d), lambda i, j: (0, 0))],
        out_specs=pl.BlockSpec((tt, d), lambda i, j: (i, 0)),
        out_shape=jax.ShapeDtypeStruct((t, d), F32),
        scratch_shapes=[pltpu.VMEM((d, tt), F32), pltpu.VMEM((te, tt), BF16)],
        compiler_params=_params("parallel", "arbitrary"),
        name="peer_dense",
    )(h2d, hn_t, a, n1, b, r2, u, vt, final_g.reshape(1, d))


def _pad_odd_w_in(w):
    out = jnp.zeros((w.shape[0], ODD_COLS), w.dtype)
    for name, (src, width) in ODD_SRC.items():
        out = lax.dynamic_update_slice(out, w[:, src:src + width], (0, ODD_DST[name]))
    return out


def _block_diag_uv(w_uv):
    out = jnp.zeros((C_HEADS * C_LATENT, C_HEADS * C_HEAD_DIM), w_uv.dtype)
    for h in range(C_HEADS):
        out = lax.dynamic_update_slice(out, w_uv[h], (h * C_LATENT, h * C_HEAD_DIM))
    return out


def kernel(x, mem, mix_norm, even_w_in, even_conv_w, even_pool_w, even_pool_scale, even_w_out, odd_w_in, odd_kv_norm, odd_w_uv, odd_hg_norm, odd_w_out, hgrn_gamma, rel_bias, mem_norm, xattn_norm, xattn_wq, xattn_wkv, xattn_wo, peer_norm, peer_wq, peer_keys, peer_u, peer_v, final_norm):
    bsz, s, d = x.shape
    t = bsz * s
    depth = mix_norm.shape[0]
    mem2d = mem.reshape(-1, d)
    h = x
    for l in range(depth):
        j = l // 2
        if l % 2 == 0:
            h = even_mixer(h, mix_norm[l], even_w_in[j].astype(BF16), even_conv_w[j].reshape(3, A_WIDTH),
                           even_pool_w[j].astype(BF16), even_pool_scale[j].reshape(1, -1),
                           even_w_out[j].astype(BF16))
        else:
            proj = norm_matmul(h.reshape(t, d), mix_norm[l], _pad_odd_w_in(odd_w_in[j]).astype(BF16), F32)
            proj = proj.reshape(bsz, s, ODD_COLS)
            y_c = dsa_attention(proj, rel_bias, odd_kv_norm[j], _block_diag_uv(odd_w_uv[j]).astype(BF16))
            y_d = hgrn2(proj, hgrn_gamma, odd_hg_norm[j], l)
            h = outproj_residual(h.reshape(t, d), y_c.reshape(t, -1), y_d.reshape(t, -1),
                                 odd_w_out[j].astype(BF16)).reshape(bsz, s, d)
        kv = norm_matmul(mem2d, mem_norm, xattn_wkv[l].astype(BF16), BF16).reshape(bsz, -1, 2 * d)
        h = memory_cross_attn(h, xattn_norm[l], xattn_wq[l].astype(BF16), kv, xattn_wo[l].astype(BF16))
        h2d = h.reshape(t, d)
        hn, a, n1, b, r2 = peer_route(h2d, peer_norm[l], peer_wq[l].astype(BF16), peer_keys[l].astype(BF16))
        h = peer_dense(h2d, hn, a, n1, b, r2, peer_u[l].astype(BF16), peer_v[l].T.astype(BF16),
                       final_norm, l == depth - 1).reshape(bsz, s, d)
    return h
```
